```python
import jax, jax.numpy as jnp
from jax import lax
import numpy as np

D_MODEL = 2048
BATCH = 1
SEQ = 16384
DEPTH = 2

CTX_LEN = 256
GRID_W = 64
HEAD_DIM = 128
NA_HEADS = 6
NA_KH = 8
NA_KW = 16
GA_HEADS = 6
GA_KV_HEADS = 2
WA_HEADS = 4
WA_KV_HEADS = 2
WINDOW = 128
Q_BLOCK = 128
ROPE_THETA = 10000.0
NORM_EPS = 1e-6
NEG_INF = -1e30
PEER_HEADS = 8
PEER_NKEYS = 128
PEER_DKEY = 256
PEER_TOPK = 16
PEER_EXPERTS = PEER_NKEYS * PEER_NKEYS
PEER_BLOCK = 128

A_W = NA_HEADS * HEAD_DIM
B_QW = GA_HEADS * HEAD_DIM
B_KW = GA_KV_HEADS * HEAD_DIM
C_QW = WA_HEADS * HEAD_DIM
C_KW = WA_KV_HEADS * HEAD_DIM
N_BRANCH = 3
N_MOD = 6
IN_SPLITS = (A_W, A_W, A_W, B_QW, B_KW, B_KW, C_QW, C_KW, C_KW, N_BRANCH * D_MODEL)
IN_COLS = 3 * A_W + B_QW + 2 * B_KW + C_QW + 2 * C_KW + N_BRANCH * D_MODEL

kernel_name = "hybrid_natten_gqa_swa_peer_dit"


def rmsnorm(x, g):
    xf = x.astype(jnp.float32)
    y = xf * lax.rsqrt(jnp.mean(xf * xf, axis=-1, keepdims=True) + NORM_EPS)
    return (y * g.astype(jnp.float32)).astype(x.dtype)


def modulate(x, shift, scale):
    return x * (1 + scale) + shift


def softmax_f32(logits):
    return jax.nn.softmax(logits.astype(jnp.float32), axis=-1)


def rope_2d(x, pos_row, pos_col):
    quarter = HEAD_DIM // 4
    half = HEAD_DIM // 2
    inv_freq = ROPE_THETA ** (-jnp.arange(quarter, dtype=jnp.float32) / quarter)

    def rot(xh, pos):
        ang = pos[:, None] * inv_freq[None, :]
        cos = jnp.cos(ang)[None, :, None, :].astype(x.dtype)
        sin = jnp.sin(ang)[None, :, None, :].astype(x.dtype)
        x1, x2 = jnp.split(xh, 2, axis=-1)
        return jnp.concatenate([x1 * cos - x2 * sin, x2 * cos + x1 * sin], axis=-1)

    return jnp.concatenate([rot(x[..., :half], pos_row), rot(x[..., half:], pos_col)], axis=-1)


def ctx_attention(q, k, v, sink=None):
    B, C, Hq, dh = q.shape
    Hkv = k.shape[2]
    G = Hq // Hkv
    qg = q.reshape(B, C, Hkv, G, dh) * (dh ** -0.5)
    s = jnp.einsum('bqkgd,bskd->bkgqs', qg, k).astype(jnp.float32)
    if sink is not None:
        sk = jnp.broadcast_to(sink.reshape(Hkv, G)[None, :, :, None, None].astype(jnp.float32), s.shape[:-1] + (1,))
        p = softmax_f32(jnp.concatenate([s, sk], axis=-1))[..., :-1]
    else:
        p = softmax_f32(s)
    o = jnp.einsum('bkgqs,bskd->bqkgd', p.astype(v.dtype), v)
    return o.reshape(B, C, Hq * dh)


def neighbourhood_attention(q, k, v, kc, vc, rpb):
    B, S, H, dh = q.shape
    rows = S // GRID_W
    kh = min(NA_KH, rows)
    qg = (q * (dh ** -0.5)).reshape(B, rows, GRID_W, H, dh)
    kg = k.reshape(B, rows, GRID_W, H, dh)
    vg = v.reshape(B, rows, GRID_W, H, dh)
    row_start = jnp.clip(jnp.arange(rows) - kh // 2, 0, rows - kh)
    col_start = jnp.clip(jnp.arange(GRID_W) - NA_KW // 2, 0, GRID_W - NA_KW)
    col_idx = col_start[:, None] + jnp.arange(NA_KW)[None, :]
    col_off = col_idx - jnp.arange(GRID_W)[:, None] + (NA_KW - 1)
    n_nb = kh * NA_KW

    def one_row(args):
        q_r, r, rs = args
        k_rows = lax.dynamic_slice_in_dim(kg, rs, kh, axis=1)
        v_rows = lax.dynamic_slice_in_dim(vg, rs, kh, axis=1)
        k_nb = k_rows[:, :, col_idx]
        v_nb = v_rows[:, :, col_idx]
        row_off = rs + jnp.arange(kh) - r + (NA_KH - 1)
        bias = rpb[:, row_off[None, :, None], col_off[:, None, :]]
        s_nb = jnp.einsum('bqhd,baqkhd->bhqak', q_r, k_nb).astype(jnp.float32) + bias[None].astype(jnp.float32)
        s_nb = s_nb.reshape(B, H, GRID_W, n_nb)
        s_ctx = jnp.einsum('bqhd,bchd->bhqc', q_r, kc).astype(jnp.float32)
        p = softmax_f32(jnp.concatenate([s_nb, s_ctx], axis=-1))
        p_nb = p[..., :n_nb].reshape(B, H, GRID_W, kh, NA_KW).astype(v.dtype)
        p_ctx = p[..., n_nb:].astype(v.dtype)
        return (jnp.einsum('bhqak,baqkhd->bqhd', p_nb, v_nb)
                + jnp.einsum('bhqc,bchd->bqhd', p_ctx, vc))

    out = lax.map(one_row, (jnp.moveaxis(qg, 1, 0), jnp.arange(rows), row_start))
    return jnp.moveaxis(out, 0, 1).reshape(B, S, H * dh)


def global_attention(q, k, v, kc, vc):
    B, S, Hq, dh = q.shape
    Hkv = k.shape[2]
    G = Hq // Hkv
    keys = jnp.concatenate([k, kc], axis=1)
    vals = jnp.concatenate([v, vc], axis=1)
    qb = (q * (dh ** -0.5)).reshape(B, S // Q_BLOCK, Q_BLOCK, Hkv, G, dh)

    def one_block(q_blk):
        s = jnp.einsum('bqkgd,bskd->bkgqs', q_blk, keys).astype(jnp.float32)
        p = softmax_f32(s).astype(vals.dtype)
        return jnp.einsum('bkgqs,bskd->bqkgd', p, vals)

    out = lax.map(one_block, jnp.moveaxis(qb, 1, 0))
    return jnp.moveaxis(out, 0, 1).reshape(B, S, Hq * dh)


def window_attention(q, k, v, kc, vc, sink):
    B, S, Hq, dh = q.shape
    Hkv = k.shape[2]
    G = Hq // Hkv
    C = kc.shape[1]
    nb = S // Q_BLOCK
    span = Q_BLOCK + 2 * WINDOW
    pad = ((0, 0), (WINDOW, WINDOW), (0, 0), (0, 0))
    kp = jnp.pad(k, pad)
    vp = jnp.pad(v, pad)
    qb = (q * (dh ** -0.5)).reshape(B, nb, Q_BLOCK, Hkv, G, dh)
    rel = jnp.arange(span)[None, :] - WINDOW - jnp.arange(Q_BLOCK)[:, None]
    in_band = jnp.abs(rel) <= WINDOW
    sink_l = sink.reshape(Hkv, G).astype(jnp.float32)

    def one_block(args):
        q_blk, i = args
        start = i * Q_BLOCK
        k_blk = lax.dynamic_slice_in_dim(kp, start, span, axis=1)
        v_blk = lax.dynamic_slice_in_dim(vp, start, span, axis=1)
        key_pos = start - WINDOW + jnp.arange(span)
        valid = in_band & ((key_pos >= 0) & (key_pos < S))[None, :]
        s_win = jnp.einsum('bqkgd,bskd->bkgqs', q_blk, k_blk).astype(jnp.float32)
        s_win = jnp.where(valid, s_win, NEG_INF)
        s_ctx = jnp.einsum('bqkgd,bskd->bkgqs', q_blk, kc).astype(jnp.float32)
        s_sink = jnp.broadcast_to(sink_l[None, :, :, None, None], s_win.shape[:-1] + (1,))
        p = softmax_f32(jnp.concatenate([s_win, s_ctx, s_sink], axis=-1))
        p_win = p[..., :span].astype(v.dtype)
        p_ctx = p[..., span:span + C].astype(v.dtype)
        return (jnp.einsum('bkgqs,bskd->bqkgd', p_win, v_blk)
                + jnp.einsum('bkgqs,bskd->bqkgd', p_ctx, vc))

    out = lax.map(one_block, (jnp.moveaxis(qb, 1, 0), jnp.arange(nb)))
    return jnp.moveaxis(out, 0, 1).reshape(B, S, Hq * dh)


def split_cols(p):
    offs = [int(o) for o in np.cumsum(IN_SPLITS)[:-1]]
    return jnp.split(p, offs, axis=-1)


def project(h, w_in, qnorm_b, knorm_b):
    B, T, _ = h.shape
    qa, ka, va, qb, kb, vb, qc, kc, vc, gates = split_cols(h @ w_in)

    def heads(t, n):
        return t.reshape(B, T, n, HEAD_DIM)

    return (heads(qa, NA_HEADS), heads(ka, NA_HEADS), heads(va, NA_HEADS),
            rmsnorm(heads(qb, GA_HEADS), qnorm_b), rmsnorm(heads(kb, GA_KV_HEADS), knorm_b), heads(vb, GA_KV_HEADS),
            heads(qc, WA_HEADS), heads(kc, WA_KV_HEADS), heads(vc, WA_KV_HEADS), gates)


def merge_branches(o_a, o_b, o_c, gates, w_br_a, w_br_b, w_br_c, w_out):
    B, T, _ = o_a.shape
    g = jax.nn.sigmoid(gates.astype(jnp.float32)).astype(o_a.dtype).reshape(B, T, N_BRANCH, D_MODEL)
    m = g[:, :, 0] * (o_a @ w_br_a) + g[:, :, 1] * (o_b @ w_br_b) + g[:, :, 2] * (o_c @ w_br_c)
    return m @ w_out


def token_mixers(h, hc, w_in, rpb_a, qnorm_b, knorm_b, sink_c, w_br_a, w_br_b, w_br_c, w_out,
                 pos_row, pos_col, ctx_out):
    qa, ka, va, qb, kb, vb, qc, kc, vc, g = project(h, w_in, qnorm_b, knorm_b)
    qa_c, ka_c, va_c, qb_c, kb_c, vb_c, qc_c, kc_c, vc_c, g_c = project(hc, w_in, qnorm_b, knorm_b)
    o_a = neighbourhood_attention(qa, ka, va, ka_c, va_c, rpb_a)
    o_b = global_attention(rope_2d(qb, pos_row, pos_col), rope_2d(kb, pos_row, pos_col), vb, kb_c, vb_c)
    o_c = window_attention(rope_2d(qc, pos_row, pos_col), rope_2d(kc, pos_row, pos_col), vc, kc_c, vc_c, sink_c)
    y = merge_branches(o_a, o_b, o_c, g, w_br_a, w_br_b, w_br_c, w_out)
    if not ctx_out:
        return y, None
    oc_a = ctx_attention(qa_c, ka_c, va_c)
    oc_b = ctx_attention(qb_c, kb_c, vb_c)
    oc_c = ctx_attention(qc_c, kc_c, vc_c, sink_c)
    yc = merge_branches(oc_a, oc_b, oc_c, g_c, w_br_a, w_br_b, w_br_c, w_out)
    return y, yc


def peer_ffn(h, wq, keys, u, v):
    B, T, D = h.shape
    nb = T // PEER_BLOCK
    half = PEER_DKEY // 2
    hb = jnp.moveaxis(h.reshape(B, nb, PEER_BLOCK, D), 1, 0)

    def one_block(h_blk):
        q = (h_blk @ wq).reshape(B, PEER_BLOCK, PEER_HEADS, 2, half)
        s = jnp.einsum('bthpd,hpnd->bthpn', q, keys).astype(jnp.float32)
        s1, i1 = lax.top_k(s[..., 0, :], PEER_TOPK)
        s2, i2 = lax.top_k(s[..., 1, :], PEER_TOPK)
        cand_s = (s1[..., :, None] + s2[..., None, :]).reshape(B, PEER_BLOCK, PEER_HEADS, PEER_TOPK * PEER_TOPK)
        cand_i = (i1[..., :, None] * PEER_NKEYS + i2[..., None, :]).reshape(B, PEER_BLOCK, PEER_HEADS, PEER_TOPK * PEER_TOPK)
        top_s, pos = lax.top_k(cand_s, PEER_TOPK)
        idx = jnp.take_along_axis(cand_i, pos, axis=-1)
        gw = softmax_f32(top_s).astype(h.dtype)
        ue = u[idx]
        act = jax.nn.gelu(jnp.einsum('btd,bthkd->bthk', h_blk, ue))
        ve = v[idx]
        return jnp.einsum('bthk,bthkd->btd', gw * act, ve)

    out = lax.map(one_block, hb)
    return jnp.moveaxis(out, 0, 1).reshape(B, T, D)


def setup_inputs(seed: int = 0) -> dict:
    key = jax.random.key(seed)
    ks = jax.random.split(key, 22)
    L, D = DEPTH, D_MODEL

    def nrm(k, shape, scale):
        return jax.random.normal(k, shape, jnp.float32) * scale

    return {
        "x": nrm(ks[0], (BATCH, SEQ, D), 1.0),
        "c": nrm(ks[1], (BATCH, D), 1.0),
        "ctx": nrm(ks[2], (BATCH, CTX_LEN, D), 1.0),
        "c_ctx": nrm(ks[3], (D,), 1.0),
        "w_mod": nrm(ks[4], (L, D, N_MOD * D), 0.5 * D ** -0.5),
        "b_mod": nrm(ks[5], (L, N_MOD * D), 0.02),
        "norm1_g": 1.0 + nrm(ks[6], (L, D), 0.02),
        "norm2_g": 1.0 + nrm(ks[7], (L, D), 0.02),
        "w_in": nrm(ks[8], (L, D, IN_COLS), D ** -0.5),
        "rpb_a": nrm(ks[9], (L, NA_HEADS, 2 * NA_KH - 1, 2 * NA_KW - 1), 0.1),
        "qnorm_b": 1.0 + nrm(ks[10], (L, HEAD_DIM), 0.02),
        "knorm_b": 1.0 + nrm(ks[11], (L, HEAD_DIM), 0.02),
        "sink_c": nrm(ks[12], (L, WA_HEADS), 0.5),
        "w_br_a": nrm(ks[13], (L, A_W, D), A_W ** -0.5),
        "w_br_b": nrm(ks[14], (L, B_QW, D), B_QW ** -0.5),
        "w_br_c": nrm(ks[15], (L, C_QW, D), C_QW ** -0.5),
        "w_out": nrm(ks[16], (L, D, D), D ** -0.5),
        "peer_wq": nrm(ks[17], (L, D, PEER_HEADS * PEER_DKEY), D ** -0.5),
        "peer_keys": nrm(ks[18], (L, PEER_HEADS, 2, PEER_NKEYS, PEER_DKEY // 2), (PEER_DKEY // 2) ** -0.5),
        "peer_u": nrm(ks[19], (L, PEER_EXPERTS, D), D ** -0.5),
        "peer_v": nrm(ks[20], (L, PEER_EXPERTS, D), 0.5),
        "final_g": 1.0 + nrm(ks[21], (D,), 0.02),
    }


def reference(x, c, ctx, c_ctx, w_mod, b_mod, norm1_g, norm2_g, w_in, rpb_a, qnorm_b, knorm_b, sink_c,
              w_br_a, w_br_b, w_br_c, w_out, peer_wq, peer_keys, peer_u, peer_v, final_g):
    B, S, D = x.shape
    C = ctx.shape[1]
    t = jnp.arange(S)
    pos_row = (t // GRID_W).astype(jnp.float32)
    pos_col = (t % GRID_W).astype(jnp.float32)
    for l in range(DEPTH):
        ctx_out = l < DEPTH - 1
        mod = jax.nn.silu(c) @ w_mod[l] + b_mod[l]
        mod_c = jax.nn.silu(c_ctx) @ w_mod[l] + b_mod[l]
        sh1, sc1, g1, sh2, sc2, g2 = [m[:, None, :] for m in jnp.split(mod, N_MOD, axis=-1)]
        sh1c, sc1c, g1c, sh2c, sc2c, g2c = jnp.split(mod_c, N_MOD, axis=-1)
        h = modulate(rmsnorm(x, norm1_g[l]), sh1, sc1)
        hc = modulate(rmsnorm(ctx, norm1_g[l]), sh1c, sc1c)
        y, yc = token_mixers(h, hc, w_in[l], rpb_a[l], qnorm_b[l], knorm_b[l], sink_c[l],
                             w_br_a[l], w_br_b[l], w_br_c[l], w_out[l], pos_row, pos_col, ctx_out)
        x = x + g1 * y
        h2 = modulate(rmsnorm(x, norm2_g[l]), sh2, sc2)
        if ctx_out:
            ctx = ctx + g1c * yc
            h2c = modulate(rmsnorm(ctx, norm2_g[l]), sh2c, sc2c)
            f = peer_ffn(jnp.concatenate([h2c, h2], axis=1), peer_wq[l], peer_keys[l], peer_u[l], peer_v[l])
            ctx = ctx + g2c * f[:, :C]
            x = x + g2 * f[:, C:]
        else:
            x = x + g2 * peer_ffn(h2, peer_wq[l], peer_keys[l], peer_u[l], peer_v[l])
    return rmsnorm(x, final_g)
```

```python
import functools

import jax
import jax.numpy as jnp
import numpy as np
from jax import lax
from jax.experimental import pallas as pl
from jax.experimental.pallas import tpu as pltpu

F32 = jnp.float32
BF16 = jnp.bfloat16

GRID_W = 64
HEAD_DIM = 128
NA_HEADS = 6
NA_KH = 8
NA_KW = 16
GA_HEADS = 6
GA_KV_HEADS = 2
WA_HEADS = 4
WA_KV_HEADS = 2
WINDOW = 128
ROPE_THETA = 10000.0
NORM_EPS = 1e-6
NEG = -1e30
PEER_HEADS = 8
PEER_NKEYS = 128
PEER_TOPK = 16
N_MOD = 6

A_W = NA_HEADS * HEAD_DIM
B_QW = GA_HEADS * HEAD_DIM
B_KW = GA_KV_HEADS * HEAD_DIM
C_QW = WA_HEADS * HEAD_DIM
C_KW = WA_KV_HEADS * HEAD_DIM

V7X_VMEM_BYTES = 64 * 1024 * 1024
VMEM_LIMIT = V7X_VMEM_BYTES - 8 * 1024 * 1024
LANE = 128

LOCAL_BLOCK = 256


def _cparams(sem):
    return pltpu.CompilerParams(dimension_semantics=sem, vmem_limit_bytes=VMEM_LIMIT)


def _dot(a, b):
    return jnp.dot(a, b, preferred_element_type=F32)


def _dot_nt(a, b):
    return lax.dot_general(a, b, (((1,), (1,)), ((), ())), preferred_element_type=F32)


def _norm_mod(x, g, sh, sc):
    ms = jnp.mean(x * x, axis=-1, keepdims=True)
    y = x * lax.rsqrt(ms + NORM_EPS) * g
    return y * (1.0 + sc) + sh


def _mod_kernel(c_ref, w_ref, b_ref, o_ref):
    c = c_ref[...]
    a = c / (1.0 + jnp.exp(-c))
    o_ref[...] = jnp.dot(a, w_ref[...], preferred_element_type=F32,
                         precision=lax.Precision.HIGHEST) + b_ref[...]


def _mod_vectors(c8, w_mod, b_mod):
    d, n = w_mod.shape
    tn = 1024
    return pl.pallas_call(
        _mod_kernel,
        out_shape=jax.ShapeDtypeStruct((8, n), F32),
        grid=(n // tn,),
        in_specs=[pl.BlockSpec((8, d), lambda j: (0, 0)),
                  pl.BlockSpec((d, tn), lambda j: (0, j)),
                  pl.BlockSpec((1, tn), lambda j: (0, j))],
        out_specs=pl.BlockSpec((8, tn), lambda j: (0, j)),
        compiler_params=_cparams(("arbitrary",)),
        name="mod_vectors",
    )(c8, w_mod, b_mod.reshape(1, n))


def _proj_kernel(x_ref, g_ref, sh_ref, sc_ref, w_ref, o_ref, h_scr):
    @pl.when(pl.program_id(1) == 0)
    def _():
        h_scr[...] = _norm_mod(x_ref[...], g_ref[...], sh_ref[...], sc_ref[...]).astype(BF16)

    o_ref[...] = _dot(h_scr[...], w_ref[...]).astype(o_ref.dtype)


def _norm_proj(x, g, sh, sc, w, tm, tn):
    r, d = x.shape
    n = w.shape[1]
    vec = pl.BlockSpec((1, d), lambda i, j: (0, 0))
    return pl.pallas_call(
        _proj_kernel,
        out_shape=jax.ShapeDtypeStruct((r, n), BF16),
        grid=(r // tm, n // tn),
        in_specs=[pl.BlockSpec((tm, d), lambda i, j: (i, 0)), vec, vec, vec,
                  pl.BlockSpec((d, tn), lambda i, j: (0, j))],
        out_specs=pl.BlockSpec((tm, tn), lambda i, j: (i, j)),
        scratch_shapes=[pltpu.VMEM((tm, d), BF16)],
        compiler_params=_cparams(("parallel", "arbitrary")),
        name="norm_proj",
    )(x, g, sh, sc, w)


def _rope_kernel(p0_ref, p1_ref, p2_ref, cos_ref, sin_ref, qn_ref, kn_ref, o_ref):
    cos = cos_ref[...]
    sin = sin_ref[...]
    lane = lax.broadcasted_iota(jnp.int32, cos.shape, 1)
    low = (lane % (HEAD_DIM // 2)) < (HEAD_DIM // 4)
    scale = HEAD_DIM ** -0.5
    src = jnp.concatenate([p0_ref[...], p1_ref[...], p2_ref[...]], axis=1).astype(F32)

    def head(col):
        return src[:, col * HEAD_DIM:(col + 1) * HEAD_DIM]

    def rms(x, g):
        ms = jnp.mean(x * x, axis=-1, keepdims=True)
        return x * lax.rsqrt(ms + NORM_EPS) * g

    def rope(x):
        partner = jnp.where(low, pltpu.roll(x, HEAD_DIM - HEAD_DIM // 4, 1),
                            pltpu.roll(x, HEAD_DIM // 4, 1))
        return x * cos + partner * sin

    outs = []
    for hq in range(GA_HEADS):
        outs.append(rope(rms(head(hq), qn_ref[...])) * scale)
    for hk in range(GA_KV_HEADS):
        outs.append(rope(rms(head(GA_HEADS + hk), kn_ref[...])))
    c0 = GA_HEADS + 2 * GA_KV_HEADS
    for hq in range(WA_HEADS):
        outs.append(rope(head(c0 + hq)) * scale)
    for hk in range(WA_KV_HEADS):
        outs.append(rope(head(c0 + WA_HEADS + hk)))
    o_ref[...] = jnp.concatenate(outs, axis=1).astype(o_ref.dtype)


ROPE_OUT_HEADS = GA_HEADS + GA_KV_HEADS + WA_HEADS + WA_KV_HEADS
RO_QB, RO_KB, RO_QC, RO_KC = 0, GA_HEADS, GA_HEADS + GA_KV_HEADS, GA_HEADS + GA_KV_HEADS + WA_HEADS


def _rope_heads(proj, cos, sin, qn, kn, tm, bc_block0):
    r = proj.shape[0]
    wblk = 6 * HEAD_DIM
    pin = [pl.BlockSpec((tm, wblk), functools.partial(lambda i, b: (i, b), b=bc_block0 + t))
           for t in range(3)]
    tab = pl.BlockSpec((tm, HEAD_DIM), lambda i: (i, 0))
    vec = pl.BlockSpec((1, HEAD_DIM), lambda i: (0, 0))
    return pl.pallas_call(
        _rope_kernel,
        out_shape=jax.ShapeDtypeStruct((r, ROPE_OUT_HEADS * HEAD_DIM), BF16),
        grid=(r // tm,),
        in_specs=pin + [tab, tab, vec, vec],
        out_specs=pl.BlockSpec((tm, ROPE_OUT_HEADS * HEAD_DIM), lambda i: (i, 0)),
        compiler_params=_cparams(("parallel",)),
        name="qk_norm_rope",
    )(proj, proj, proj, cos, sin, qn, kn)


def _local_kernel(sink_ref, q_ref, k0_ref, k1_ref, k2_ref, v0_ref, v1_ref, v2_ref,
                  kc_ref, vc_ref, bias_ref, o_ref, *, scale):
    h = pl.program_id(0)
    q = q_ref[...]
    s = jnp.concatenate([_dot_nt(q, k0_ref[...]), _dot_nt(q, k1_ref[...]),
                         _dot_nt(q, k2_ref[...])], axis=1)
    sc = _dot_nt(q, kc_ref[...])
    if scale != 1.0:
        s = s * scale
        sc = sc * scale
    s = s + bias_ref[0, 0]
    m = jnp.maximum(jnp.max(s, axis=1, keepdims=True), jnp.max(sc, axis=1, keepdims=True))
    p = jnp.exp(s - m)
    pc = jnp.exp(sc - m)
    l = (jnp.sum(p, axis=1, keepdims=True) + jnp.sum(pc, axis=1, keepdims=True)
         + jnp.exp(sink_ref[h] - m))
    b = LOCAL_BLOCK
    pb = p.astype(BF16)
    o = (_dot(pb[:, 0:b], v0_ref[...]) + _dot(pb[:, b:2 * b], v1_ref[...])
         + _dot(pb[:, 2 * b:3 * b], v2_ref[...]) + _dot(pc.astype(BF16), vc_ref[...]))
    o_ref[...] = (o / l).astype(o_ref.dtype)


def _local_attention(q_arr, q_col0, k_arr, k_col0, v_arr, v_col0, kc_arr, kc_col0, vc_arr,
                     vc_col0, bias, sink, n_heads, group, scale):
    s_len = q_arr.shape[0]
    c_len = kc_arr.shape[0]
    b = LOCAL_BLOCK
    nb = s_len // b
    hb = bias.shape[1]

    def kv_spec(col0, off):
        return pl.BlockSpec(
            (b, HEAD_DIM), lambda h, i: (jnp.clip(i + off, 0, nb - 1), col0 + h // group))

    def variant(i):
        return jnp.where(i == 0, 0, jnp.where(i == nb - 1, 2, 1))

    in_specs = [
        pl.BlockSpec(memory_space=pltpu.SMEM),
        pl.BlockSpec((b, HEAD_DIM), lambda h, i: (i, q_col0 + h)),
        kv_spec(k_col0, -1), kv_spec(k_col0, 0), kv_spec(k_col0, 1),
        kv_spec(v_col0, -1), kv_spec(v_col0, 0), kv_spec(v_col0, 1),
        pl.BlockSpec((c_len, HEAD_DIM), lambda h, i: (0, kc_col0 + h // group)),
        pl.BlockSpec((c_len, HEAD_DIM), lambda h, i: (0, vc_col0 + h // group)),
        pl.BlockSpec((1, 1, b, 3 * b),
                     lambda h, i: (variant(i), h if hb > 1 else 0, 0, 0)),
    ]
    return pl.pallas_call(
        functools.partial(_local_kernel, scale=scale),
        out_shape=jax.ShapeDtypeStruct((s_len, n_heads * HEAD_DIM), BF16),
        grid=(n_heads, nb),
        in_specs=in_specs,
        out_specs=pl.BlockSpec((b, HEAD_DIM), lambda h, i: (i, h)),
        compiler_params=_cparams(("parallel", "arbitrary")),
        name="local_attention",
    )(sink, q_arr, k_arr, k_arr, k_arr, v_arr, v_arr, v_arr, kc_arr, vc_arr, bias)


def _flash_kernel(sink_ref, q_ref, k_ref, v_ref, kc_ref, vc_ref, o_ref, m_scr, l_scr, acc_scr,
                  *, n_main, n_steps, scale):
    h = pl.program_id(0)
    j = pl.program_id(2)

    @pl.when(j == 0)
    def _():
        m_scr[...] = jnp.full(m_scr.shape, NEG, F32)
        l_scr[...] = jnp.zeros(l_scr.shape, F32)
        acc_scr[...] = jnp.zeros(acc_scr.shape, F32)

    def step(k, v):
        s = _dot_nt(q_ref[...], k)
        if scale != 1.0:
            s = s * scale
        m_prev = m_scr[...]
        m_new = jnp.maximum(m_prev, jnp.max(s, axis=1, keepdims=True))
        alpha = jnp.exp(m_prev - m_new)
        p = jnp.exp(s - m_new)
        l_scr[...] = alpha * l_scr[...] + jnp.sum(p, axis=1, keepdims=True)
        acc_scr[...] = alpha * acc_scr[...] + _dot(p.astype(BF16), v)
        m_scr[...] = m_new

    @pl.when(j < n_main)
    def _():
        step(k_ref[...], v_ref[...])

    if n_steps > n_main:
        @pl.when(j == n_main)
        def _():
            step(kc_ref[...], vc_ref[...])

    @pl.when(j == n_steps - 1)
    def _():
        l = l_scr[...] + jnp.exp(sink_ref[h] - m_scr[...])
        o_ref[...] = (acc_scr[...] / l).astype(o_ref.dtype)


def _flash_attention(q_arr, q_col0, k_arr, k_col0, v_arr, v_col0, kc_arr, kc_col0, vc_arr,
                     vc_col0, sink, n_heads, group, scale, tq, tk, with_ctx):
    s_len = q_arr.shape[0]
    kv_len = k_arr.shape[0]
    c_len = kc_arr.shape[0]
    n_main = kv_len // tk
    n_steps = n_main + (1 if with_ctx else 0)
    last = n_main - 1
    in_specs = [
        pl.BlockSpec(memory_space=pltpu.SMEM),
        pl.BlockSpec((tq, HEAD_DIM), lambda h, i, j: (i, q_col0 + h)),
        pl.BlockSpec((tk, HEAD_DIM), lambda h, i, j: (jnp.minimum(j, last), k_col0 + h // group)),
        pl.BlockSpec((tk, HEAD_DIM), lambda h, i, j: (jnp.minimum(j, last), v_col0 + h // group)),
        pl.BlockSpec((c_len, HEAD_DIM), lambda h, i, j: (0, kc_col0 + h // group)),
        pl.BlockSpec((c_len, HEAD_DIM), lambda h, i, j: (0, vc_col0 + h // group)),
    ]
    return pl.pallas_call(
        functools.partial(_flash_kernel, n_main=n_main, n_steps=n_steps, scale=scale),
        out_shape=jax.ShapeDtypeStruct((s_len, n_heads * HEAD_DIM), BF16),
        grid=(n_heads, s_len // tq, n_steps),
        in_specs=in_specs,
        out_specs=pl.BlockSpec((tq, HEAD_DIM), lambda h, i, j: (i, h)),
        scratch_shapes=[pltpu.VMEM((tq, 1), F32), pltpu.VMEM((tq, 1), F32),
                        pltpu.VMEM((tq, HEAD_DIM), F32)],
        compiler_params=_cparams(("parallel", "parallel", "arbitrary")),
        name="flash_attention",
    )(sink, q_arr, k_arr, v_arr, kc_arr, vc_arr)


def _merge_kernel(oa_ref, ob_ref, oc_ref, g0_ref, g1_ref, g2_ref, wa_ref, wb_ref, wc_ref,
                  wo_ref, x_ref, gate_ref, o_ref):
    def sig(g_ref):
        g = g_ref[...].astype(F32)
        return 1.0 / (1.0 + jnp.exp(-g))

    m = (sig(g0_ref) * _dot(oa_ref[...], wa_ref[...])
         + sig(g1_ref) * _dot(ob_ref[...], wb_ref[...])
         + sig(g2_ref) * _dot(oc_ref[...], wc_ref[...]))
    y = _dot(m.astype(BF16), wo_ref[...])
    o_ref[...] = x_ref[...] + gate_ref[...] * y


def _merge(oa, ob, oc, proj, wa, wb, wc, wo, x, gate, tm):
    r, d = x.shape

    def rows(w):
        return pl.BlockSpec((tm, w), lambda i: (i, 0))

    def full(a):
        return pl.BlockSpec(a.shape, lambda i: (0, 0))

    gates = [pl.BlockSpec((tm, d), functools.partial(lambda i, b: (i, b), b=t)) for t in range(3)]
    return pl.pallas_call(
        _merge_kernel,
        out_shape=jax.ShapeDtypeStruct((r, d), F32),
        grid=(r // tm,),
        in_specs=[rows(oa.shape[1]), rows(ob.shape[1]), rows(oc.shape[1])] + gates
        + [full(wa), full(wb), full(wc), full(wo), rows(d), pl.BlockSpec((1, d), lambda i: (0, 0))],
        out_specs=rows(d),
        compiler_params=_cparams(("parallel",)),
        name="merge_branches",
    )(oa, ob, oc, proj, proj, proj, wa, wb, wc, wo, x, gate)


def _topk_ranks(s):
    n_rows = s.shape[0]
    row = lax.broadcasted_iota(jnp.int32, s.shape, 0)
    rank = jnp.full(s.shape, float(PEER_TOPK), F32)
    vals = []
    for r in range(PEER_TOPK):
        m = jnp.max(s, axis=0, keepdims=True)
        first = jnp.min(jnp.where(s == m, row, n_rows), axis=0, keepdims=True)
        sel = row == first
        rank = jnp.where(sel, float(r), rank)
        s = jnp.where(sel, -jnp.inf, s)
        vals.append(m)
    return rank, vals


def _peer_select_kernel(x_ref, g_ref, sh_ref, sc_ref, wq_ref, keys_ref,
                        ht_ref, n1_ref, a_ref, rk2_ref, b_ref):
    h2 = _norm_mod(x_ref[...], g_ref[...], sh_ref[...], sc_ref[...])
    h2t = h2.T.astype(BF16)
    ht_ref[...] = h2t
    qt = _dot(wq_ref[...], h2t)
    k = PEER_TOPK
    for h in range(PEER_HEADS):
        sv = []
        for p in range(2):
            hp = 2 * h + p
            sv.append(jnp.dot(keys_ref[hp], qt[hp * PEER_NKEYS:(hp + 1) * PEER_NKEYS, :],
                              preferred_element_type=F32, precision=lax.Precision.HIGHEST))
        rank1, d1 = _topk_ranks(sv[0])
        rank2, d2 = _topk_ranks(sv[1])
        d2_all = jnp.concatenate(d2, axis=0)
        cand = jnp.concatenate([d1[a] + d2_all for a in range(k)], axis=0)
        rank_c, _ = _topk_ranks(cand)
        chosen = rank_c < float(k)
        c_max = d1[0] + d2[0]
        z = jnp.sum(jnp.where(chosen, jnp.exp(cand - c_max), 0.0), axis=0, keepdims=True)
        n1 = jnp.zeros(rank1.shape, F32)
        for a in range(k):
            n_a = jnp.sum(jnp.where(chosen[a * k:(a + 1) * k], 1.0, 0.0), axis=0, keepdims=True)
            n1 = n1 + jnp.where(rank1 == float(a), n_a, 0.0)
        n1_ref[h] = n1
        a_ref[h] = jnp.exp(sv[0] - d1[0]) / z
        rk2_ref[h] = rank2
        b_ref[h] = jnp.exp(sv[1] - d2[0])


def _peer_select(x, g, sh, sc, wq_t, keys, tm):
    r, d = x.shape
    vec = pl.BlockSpec((1, d), lambda i: (0, 0))
    tab = jax.ShapeDtypeStruct((PEER_HEADS, PEER_NKEYS, r), F32)
    tab_spec = pl.BlockSpec((PEER_HEADS, PEER_NKEYS, tm), lambda i: (0, 0, i))
    return pl.pallas_call(
        _peer_select_kernel,
        out_shape=(jax.ShapeDtypeStruct((d, r), BF16), tab, tab, tab, tab),
        grid=(r // tm,),
        in_specs=[pl.BlockSpec((tm, d), lambda i: (i, 0)), vec, vec, vec,
                  pl.BlockSpec(wq_t.shape, lambda i: (0, 0)),
                  pl.BlockSpec(keys.shape, lambda i: (0, 0, 0))],
        out_specs=(pl.BlockSpec((d, tm), lambda i: (0, i)), tab_spec, tab_spec, tab_spec, tab_spec),
        compiler_params=_cparams(("parallel",)),
        name="peer_select",
    )(x, g, sh, sc, wq_t, keys)


def _peer_expert_kernel(ht_ref, u_ref, vt_ref, n1_ref, a_ref, rk2_ref, b_ref, x_ref, gate_ref,
                        o_ref, acc_scr, c_scr, *, n_chunk):
    kk = pl.program_id(1)

    @pl.when(kk == 0)
    def _():
        acc_scr[...] = jnp.zeros(acc_scr.shape, F32)

    act_t = _dot(u_ref[...], ht_ref[...])
    for c in range(n_chunk):
        i1 = kk * n_chunk + c
        coef = None
        for h in range(PEER_HEADS):
            n = n1_ref[h, pl.ds(i1, 1), :]
            a = a_ref[h, pl.ds(i1, 1), :]
            term = a * jnp.where(rk2_ref[h] < n, b_ref[h], 0.0)
            coef = term if coef is None else coef + term
        act = act_t[c * PEER_NKEYS:(c + 1) * PEER_NKEYS, :]
        c_scr[c * PEER_NKEYS:(c + 1) * PEER_NKEYS, :] = (jax.nn.gelu(act) * coef).astype(BF16)
    acc_scr[...] += _dot(vt_ref[...], c_scr[...])

    @pl.when(kk == pl.num_programs(1) - 1)
    def _():
        o_ref[...] = x_ref[...] + gate_ref[...] * acc_scr[...].T


def _peer_experts(h_t, n1, a, rk2, b, u, v_t, x, gate, tm, ec):
    d, r = h_t.shape
    e = u.shape[0]
    n_chunk = ec // PEER_NKEYS
    tab_spec = pl.BlockSpec((PEER_HEADS, PEER_NKEYS, tm), lambda i, k: (0, 0, i))
    return pl.pallas_call(
        functools.partial(_peer_expert_kernel, n_chunk=n_chunk),
        out_shape=jax.ShapeDtypeStruct((r, d), F32),
        grid=(r // tm, e // ec),
        in_specs=[pl.BlockSpec((d, tm), lambda i, k: (0, i)),
                  pl.BlockSpec((ec, d), lambda i, k: (k, 0)),
                  pl.BlockSpec((d, ec), lambda i, k: (0, k)),
                  tab_spec, tab_spec, tab_spec, tab_spec,
                  pl.BlockSpec((tm, d), lambda i, k: (i, 0)),
                  pl.BlockSpec((1, d), lambda i, k: (0, 0))],
        out_specs=pl.BlockSpec((tm, d), lambda i, k: (i, 0)),
        scratch_shapes=[pltpu.VMEM((d, tm), F32), pltpu.VMEM((ec, tm), BF16)],
        compiler_params=_cparams(("parallel", "arbitrary")),
        name="peer_experts",
    )(h_t, u, v_t, n1, a, rk2, b, x, gate)


def _final_kernel(x_ref, g_ref, o_ref):
    x = x_ref[...]
    ms = jnp.mean(x * x, axis=-1, keepdims=True)
    o_ref[...] = x * lax.rsqrt(ms + NORM_EPS) * g_ref[...]


def _final_norm(x, g, tm):
    r, d = x.shape
    return pl.pallas_call(
        _final_kernel,
        out_shape=jax.ShapeDtypeStruct((r, d), F32),
        grid=(r // tm,),
        in_specs=[pl.BlockSpec((tm, d), lambda i: (i, 0)), pl.BlockSpec((1, d), lambda i: (0, 0))],
        out_specs=pl.BlockSpec((tm, d), lambda i: (i, 0)),
        compiler_params=_cparams(("parallel",)),
        name="final_norm",
    )(x, g)


def _rope_tables(s_len):
    quarter = HEAD_DIM // 4
    t = jnp.arange(s_len)
    pos_row = (t // GRID_W).astype(F32)
    pos_col = (t % GRID_W).astype(F32)
    inv_freq = ROPE_THETA ** (-jnp.arange(quarter, dtype=F32) / quarter)
    ar = pos_row[:, None] * inv_freq[None, :]
    ac = pos_col[:, None] * inv_freq[None, :]
    cos = jnp.concatenate([jnp.cos(ar), jnp.cos(ar), jnp.cos(ac), jnp.cos(ac)], axis=1)
    sin = jnp.concatenate([-jnp.sin(ar), jnp.sin(ar), -jnp.sin(ac), jnp.sin(ac)], axis=1)
    return cos, sin


def _local_geometry(rows):
    b = LOCAL_BLOCK
    rows_per_block = b // GRID_W
    kh = min(NA_KH, rows)
    q = np.arange(b)
    k = np.arange(3 * b)
    q_row_l, q_col = q // GRID_W, q % GRID_W
    k_row_l, k_col = k // GRID_W - rows_per_block, k % GRID_W
    nb = rows // rows_per_block
    geo = []
    for block in (0, 1, nb - 1):
        q_row = block * rows_per_block + q_row_l
        k_row = block * rows_per_block + k_row_l
        geo.append((q_row, q_col, k_row, k_col))
    return kh, geo


def _na_bias(rpb, rows):
    kh, geo = _local_geometry(rows)
    out = []
    for q_row, q_col, k_row, k_col in geo:
        rs = np.clip(q_row - kh // 2, 0, rows - kh)
        cs = np.clip(q_col - NA_KW // 2, 0, GRID_W - NA_KW)
        ok = ((k_row[None, :] >= rs[:, None]) & (k_row[None, :] < rs[:, None] + kh)
              & (k_col[None, :] >= cs[:, None]) & (k_col[None, :] < cs[:, None] + NA_KW)
              & (k_row[None, :] >= 0) & (k_row[None, :] < rows))
        row_off = np.clip(k_row[None, :] - q_row[:, None] + (NA_KH - 1), 0, 2 * NA_KH - 2)
        col_off = np.clip(k_col[None, :] - q_col[:, None] + (NA_KW - 1), 0, 2 * NA_KW - 2)
        bias = rpb[:, row_off, col_off]
        out.append(jnp.where(jnp.asarray(ok)[None], bias, NEG))
    return jnp.stack(out, axis=0).astype(F32)


def _window_mask(rows):
    _, geo = _local_geometry(rows)
    out = []
    for q_row, q_col, k_row, k_col in geo:
        qp = q_row * GRID_W + q_col
        kp = k_row * GRID_W + k_col
        ok = (np.abs(kp[None, :] - qp[:, None]) <= WINDOW) & (kp[None, :] >= 0) \
            & (kp[None, :] < rows * GRID_W)
        out.append(np.where(ok, 0.0, NEG)[None])
    return jnp.asarray(np.stack(out, axis=0), F32)


def _row_tile(r, pref):
    return pref if r % pref == 0 else LOCAL_BLOCK


def kernel(x, c, ctx, c_ctx, w_mod, b_mod, norm1_g, norm2_g, w_in, rpb_a, qnorm_b, knorm_b, sink_c,
           w_br_a, w_br_b, w_br_c, w_out, peer_wq, peer_keys, peer_u, peer_v, final_g):
    _, s_len, d = x.shape
    c_len = ctx.shape[1]
    depth = w_mod.shape[0]
    rows = s_len // GRID_W
    xs = x[0]
    xc = ctx[0]

    cos, sin = _rope_tables(s_len)
    cos_c = jnp.ones((c_len, HEAD_DIM), F32)
    sin_c = jnp.zeros((c_len, HEAD_DIM), F32)
    win_mask = _window_mask(rows)
    no_sink_a = jnp.full((NA_HEADS,), NEG, F32)
    no_sink_b = jnp.full((GA_HEADS,), NEG, F32)
    c8 = jnp.zeros((8, d), F32).at[0].set(c_ctx).at[1].set(c[0])

    g_blk = 3 * d // HEAD_DIM
    qa0, ka0, va0 = g_blk, g_blk + NA_HEADS, g_blk + 2 * NA_HEADS
    bc0 = g_blk + 3 * NA_HEADS
    vb0 = bc0 + GA_HEADS + GA_KV_HEADS
    vc0 = vb0 + GA_KV_HEADS + WA_HEADS + WA_KV_HEADS
    n_qkv = 3 * A_W + B_QW + 2 * B_KW + C_QW + 2 * C_KW

    tm_s = _row_tile(s_len, 512)
    gb = GA_HEADS // GA_KV_HEADS
    gc = WA_HEADS // WA_KV_HEADS
    for l in range(depth):
        ctx_out = l < depth - 1
        w_in_l = jnp.concatenate([w_in[l][:, n_qkv:], w_in[l][:, :n_qkv]], axis=1).astype(BF16)
        mod = _mod_vectors(c8, w_mod[l], b_mod[l])
        mc = [mod[0:1, t * d:(t + 1) * d] for t in range(N_MOD)]
        ms = [mod[1:2, t * d:(t + 1) * d] for t in range(N_MOD)]
        g1 = norm1_g[l].reshape(1, d)
        g2 = norm2_g[l].reshape(1, d)

        proj = _norm_proj(xs, g1, ms[0], ms[1], w_in_l, tm_s, 1536)
        proj_c = _norm_proj(xc, g1, mc[0], mc[1], w_in_l, c_len, 1536)
        qn = qnorm_b[l].reshape(1, HEAD_DIM)
        kn = knorm_b[l].reshape(1, HEAD_DIM)
        bc_block0 = bc0 // 6
        rope = _rope_heads(proj, cos, sin, qn, kn, tm_s, bc_block0)
        rope_c = _rope_heads(proj_c, cos_c, sin_c, qn, kn, c_len, bc_block0)

        na_scale = HEAD_DIM ** -0.5
        o_a = _local_attention(proj, qa0, proj, ka0, proj, va0, proj_c, ka0, proj_c, va0,
                               _na_bias(rpb_a[l], rows), no_sink_a, NA_HEADS, 1, na_scale)
        o_b = _flash_attention(rope, RO_QB, rope, RO_KB, proj, vb0, rope_c, RO_KB, proj_c, vb0,
                               no_sink_b, GA_HEADS, gb, 1.0, tm_s, _row_tile(s_len, 1024), True)
        o_c = _local_attention(rope, RO_QC, rope, RO_KC, proj, vc0, rope_c, RO_KC, proj_c, vc0,
                               win_mask, sink_c[l], WA_HEADS, gc, 1.0)
        wa = w_br_a[l].astype(BF16)
        wb = w_br_b[l].astype(BF16)
        wc = w_br_c[l].astype(BF16)
        wo = w_out[l].astype(BF16)
        x_mid = _merge(o_a, o_b, o_c, proj, wa, wb, wc, wo, xs, ms[2], LOCAL_BLOCK)

        wq_t = peer_wq[l].T.astype(BF16)
        keys = peer_keys[l].reshape(2 * PEER_HEADS, PEER_NKEYS, -1)
        u = peer_u[l].astype(BF16)
        v_t = peer_v[l].T.astype(BF16)
        sel = _peer_select(x_mid, g2, ms[3], ms[4], wq_t, keys, LOCAL_BLOCK)
        xs = _peer_experts(*sel, u=u, v_t=v_t, x=x_mid, gate=ms[5], tm=tm_s, ec=512)

        if ctx_out:
            oc_a = _flash_attention(proj_c, qa0, proj_c, ka0, proj_c, va0, proj_c, ka0, proj_c,
                                    va0, no_sink_a, NA_HEADS, 1, na_scale, c_len, c_len, False)
            oc_b = _flash_attention(rope_c, RO_QB, rope_c, RO_KB, proj_c, vb0, rope_c, RO_KB,
                                    proj_c, vb0, no_sink_b, GA_HEADS, gb, 1.0, c_len, c_len, False)
            oc_c = _flash_attention(rope_c, RO_QC, rope_c, RO_KC, proj_c, vc0, rope_c, RO_KC,
                                    proj_c, vc0, sink_c[l], WA_HEADS, gc, 1.0, c_len, c_len, False)
            xc_mid = _merge(oc_a, oc_b, oc_c, proj_c, wa, wb, wc, wo, xc, mc[2], c_len)
            sel_c = _peer_select(xc_mid, g2, mc[3], mc[4], wq_t, keys, c_len)
            xc = _peer_experts(*sel_c, u=u, v_t=v_t, x=xc_mid, gate=mc[5], tm=c_len, ec=512)

    return _final_norm(xs, final_g.reshape(1, d), tm_s).reshape(1, s_len, d)
```

```python
import functools

import jax
import jax.numpy as jnp
import numpy as np
from jax import lax
from jax.experimental import pallas as pl
from jax.experimental.pallas import tpu as pltpu

F32 = jnp.float32
BF16 = jnp.bfloat16

GRID_W = 64
HEAD_DIM = 128
NA_HEADS = 6
NA_KH = 8
NA_KW = 16
GA_HEADS = 6
GA_KV_HEADS = 2
WA_HEADS = 4
WA_KV_HEADS = 2
WINDOW = 128
ROPE_THETA = 10000.0
NORM_EPS = 1e-6
NEG = -1e30
LOG2E = 1.4426950408889634
PEER_HEADS = 8
PEER_NKEYS = 128
PEER_TOPK = 16
N_MOD = 6

A_W = NA_HEADS * HEAD_DIM
B_QW = GA_HEADS * HEAD_DIM
B_KW = GA_KV_HEADS * HEAD_DIM
C_QW = WA_HEADS * HEAD_DIM
C_KW = WA_KV_HEADS * HEAD_DIM

V7X_VMEM_BYTES = 64 * 1024 * 1024
VMEM_LIMIT = V7X_VMEM_BYTES - 8 * 1024 * 1024
LANE = 128
BF16_ROWS = 16

LOCAL_BLOCK = 256
PEER_STAGE = 512
PEER_EXPERT_CHUNK = 1024


def _cparams(sem):
    return pltpu.CompilerParams(dimension_semantics=sem, vmem_limit_bytes=VMEM_LIMIT)


def _dot(a, b):
    return jnp.dot(a, b, preferred_element_type=F32)


def _dot_nt(a, b):
    return lax.dot_general(a, b, (((1,), (1,)), ((), ())), preferred_element_type=F32)


def _norm_mod(x, g, sh, sc):
    ms = jnp.mean(x * x, axis=-1, keepdims=True)
    y = x * lax.rsqrt(ms + NORM_EPS) * g
    return y * (1.0 + sc) + sh


def _mod_kernel(c_ref, w_ref, b_ref, o_ref):
    c = c_ref[...]
    a = c / (1.0 + jnp.exp(-c))
    o_ref[...] = jnp.dot(a, w_ref[...], preferred_element_type=F32,
                         precision=lax.Precision.HIGHEST) + b_ref[...]


def _mod_vectors(c8, w_mod, b_mod):
    d, n = w_mod.shape
    tn = 1024
    return pl.pallas_call(
        _mod_kernel,
        out_shape=jax.ShapeDtypeStruct((8, n), F32),
        grid=(n // tn,),
        in_specs=[pl.BlockSpec((8, d), lambda j: (0, 0)),
                  pl.BlockSpec((d, tn), lambda j: (0, j)),
                  pl.BlockSpec((1, tn), lambda j: (0, j))],
        out_specs=pl.BlockSpec((8, tn), lambda j: (0, j)),
        compiler_params=_cparams(("arbitrary",)),
        name="mod_vectors",
    )(c8, w_mod, b_mod.reshape(1, n))


def _proj_kernel(x_ref, g_ref, sh_ref, sc_ref, w_ref, o_ref, h_scr):
    @pl.when(pl.program_id(1) == 0)
    def _():
        h_scr[...] = _norm_mod(x_ref[...], g_ref[...], sh_ref[...], sc_ref[...]).astype(BF16)

    o_ref[...] = _dot(h_scr[...], w_ref[...]).astype(o_ref.dtype)


def _norm_proj(x, g, sh, sc, w, tm, tn):
    r, d = x.shape
    n = w.shape[1]
    vec = pl.BlockSpec((1, d), lambda i, j: (0, 0))
    return pl.pallas_call(
        _proj_kernel,
        out_shape=jax.ShapeDtypeStruct((r, n), BF16),
        grid=(r // tm, n // tn),
        in_specs=[pl.BlockSpec((tm, d), lambda i, j: (i, 0)), vec, vec, vec,
                  pl.BlockSpec((d, tn), lambda i, j: (0, j))],
        out_specs=pl.BlockSpec((tm, tn), lambda i, j: (i, j)),
        scratch_shapes=[pltpu.VMEM((tm, d), BF16)],
        compiler_params=_cparams(("parallel", "arbitrary")),
        name="norm_proj",
    )(x, g, sh, sc, w)


def _rope_kernel(p0_ref, p1_ref, p2_ref, cos_ref, sin_ref, qn_ref, kn_ref, o_ref):
    cos = cos_ref[...]
    sin = sin_ref[...]
    lane = lax.broadcasted_iota(jnp.int32, cos.shape, 1)
    low = (lane % (HEAD_DIM // 2)) < (HEAD_DIM // 4)
    scale = HEAD_DIM ** -0.5
    src = jnp.concatenate([p0_ref[...], p1_ref[...], p2_ref[...]], axis=1).astype(F32)

    def head(col):
        return src[:, col * HEAD_DIM:(col + 1) * HEAD_DIM]

    def rms(x, g):
        ms = jnp.mean(x * x, axis=-1, keepdims=True)
        return x * lax.rsqrt(ms + NORM_EPS) * g

    def rope(x):
        partner = jnp.where(low, pltpu.roll(x, HEAD_DIM - HEAD_DIM // 4, 1),
                            pltpu.roll(x, HEAD_DIM // 4, 1))
        return x * cos + partner * sin

    outs = []
    for hq in range(GA_HEADS):
        outs.append(rope(rms(head(hq), qn_ref[...])) * (scale * LOG2E))
    for hk in range(GA_KV_HEADS):
        outs.append(rope(rms(head(GA_HEADS + hk), kn_ref[...])))
    c0 = GA_HEADS + 2 * GA_KV_HEADS
    for hq in range(WA_HEADS):
        outs.append(rope(head(c0 + hq)) * scale)
    for hk in range(WA_KV_HEADS):
        outs.append(rope(head(c0 + WA_HEADS + hk)))
    o_ref[...] = jnp.concatenate(outs, axis=1).astype(o_ref.dtype)


ROPE_OUT_HEADS = GA_HEADS + GA_KV_HEADS + WA_HEADS + WA_KV_HEADS
RO_QB, RO_KB, RO_QC, RO_KC = 0, GA_HEADS, GA_HEADS + GA_KV_HEADS, GA_HEADS + GA_KV_HEADS + WA_HEADS


def _rope_heads(proj, cos, sin, qn, kn, tm, bc_block0):
    r = proj.shape[0]
    wblk = 6 * HEAD_DIM
    pin = [pl.BlockSpec((tm, wblk), functools.partial(lambda i, b: (i, b), b=bc_block0 + t))
           for t in range(3)]
    tab = pl.BlockSpec((tm, HEAD_DIM), lambda i: (i, 0))
    vec = pl.BlockSpec((1, HEAD_DIM), lambda i: (0, 0))
    return pl.pallas_call(
        _rope_kernel,
        out_shape=jax.ShapeDtypeStruct((r, ROPE_OUT_HEADS * HEAD_DIM), BF16),
        grid=(r // tm,),
        in_specs=pin + [tab, tab, vec, vec],
        out_specs=pl.BlockSpec((tm, ROPE_OUT_HEADS * HEAD_DIM), lambda i: (i, 0)),
        compiler_params=_cparams(("parallel",)),
        name="qk_norm_rope",
    )(proj, proj, proj, cos, sin, qn, kn)


def _local_kernel(sink_ref, q_ref, k0_ref, k1_ref, k2_ref, v0_ref, v1_ref, v2_ref,
                  kc_ref, vc_ref, bias_ref, o_ref, *, scale):
    h = pl.program_id(0)
    q = q_ref[...]
    s = jnp.concatenate([_dot_nt(q, k0_ref[...]), _dot_nt(q, k1_ref[...]),
                         _dot_nt(q, k2_ref[...])], axis=1)
    sc = _dot_nt(q, kc_ref[...])
    if scale != 1.0:
        s = s * scale
        sc = sc * scale
    s = s + bias_ref[0, 0]
    m = jnp.maximum(jnp.max(s, axis=1, keepdims=True), jnp.max(sc, axis=1, keepdims=True))
    p = jnp.exp(s - m)
    pc = jnp.exp(sc - m)
    l = (jnp.sum(p, axis=1, keepdims=True) + jnp.sum(pc, axis=1, keepdims=True)
         + jnp.exp(sink_ref[h] - m))
    b = LOCAL_BLOCK
    pb = p.astype(BF16)
    o = (_dot(pb[:, 0:b], v0_ref[...]) + _dot(pb[:, b:2 * b], v1_ref[...])
         + _dot(pb[:, 2 * b:3 * b], v2_ref[...]) + _dot(pc.astype(BF16), vc_ref[...]))
    o_ref[...] = (o / l).astype(o_ref.dtype)


def _local_attention(q_arr, q_col0, k_arr, k_col0, v_arr, v_col0, kc_arr, kc_col0, vc_arr,
                     vc_col0, bias, sink, n_heads, group, scale):
    s_len = q_arr.shape[0]
    c_len = kc_arr.shape[0]
    b = LOCAL_BLOCK
    nb = s_len // b
    hb = bias.shape[1]

    def kv_spec(col0, off):
        return pl.BlockSpec(
            (b, HEAD_DIM), lambda h, i: (jnp.clip(i + off, 0, nb - 1), col0 + h // group))

    def variant(i):
        return jnp.where(i == 0, 0, jnp.where(i == nb - 1, 2, 1))

    in_specs = [
        pl.BlockSpec(memory_space=pltpu.SMEM),
        pl.BlockSpec((b, HEAD_DIM), lambda h, i: (i, q_col0 + h)),
        kv_spec(k_col0, -1), kv_spec(k_col0, 0), kv_spec(k_col0, 1),
        kv_spec(v_col0, -1), kv_spec(v_col0, 0), kv_spec(v_col0, 1),
        pl.BlockSpec((c_len, HEAD_DIM), lambda h, i: (0, kc_col0 + h // group)),
        pl.BlockSpec((c_len, HEAD_DIM), lambda h, i: (0, vc_col0 + h // group)),
        pl.BlockSpec((1, 1, b, 3 * b),
                     lambda h, i: (variant(i), h if hb > 1 else 0, 0, 0)),
    ]
    return pl.pallas_call(
        functools.partial(_local_kernel, scale=scale),
        out_shape=jax.ShapeDtypeStruct((s_len, n_heads * HEAD_DIM), BF16),
        grid=(n_heads, nb),
        in_specs=in_specs,
        out_specs=pl.BlockSpec((b, HEAD_DIM), lambda h, i: (i, h)),
        compiler_params=_cparams(("parallel", "arbitrary")),
        name="local_attention",
    )(sink, q_arr, k_arr, k_arr, k_arr, v_arr, v_arr, v_arr, kc_arr, vc_arr, bias)


def _flash_kernel(sink_ref, q_ref, k_ref, v_ref, kc_ref, vc_ref, o_ref, m_scr, acc_scr,
                  *, n_main, n_steps, scale, group):
    kvh = pl.program_id(0)
    j = pl.program_id(2)
    d = HEAD_DIM

    @pl.when(j == 0)
    def _():
        m_scr[...] = jnp.full(m_scr.shape, NEG, F32)
        acc_scr[...] = jnp.zeros(acc_scr.shape, F32)

    def step(k, v):
        tk = k.shape[0]
        v1 = jnp.concatenate([v, jnp.ones((tk, d), BF16)], axis=1)
        for g in range(group):
            s = _dot_nt(q_ref[:, g * d:(g + 1) * d], k)
            if scale != 1.0:
                s = s * scale
            mx = s[:, 0:d]
            for t in range(1, tk // d):
                mx = jnp.maximum(mx, s[:, t * d:(t + 1) * d])
            m_prev = m_scr[g]
            m_new = jnp.maximum(m_prev, jnp.max(mx, axis=1, keepdims=True))
            alpha = jnp.exp2(m_prev - m_new)
            p = jnp.exp2(s - jnp.concatenate([m_new] * (tk // d), axis=1))
            acc_scr[g] = (jnp.concatenate([alpha, alpha], axis=1) * acc_scr[g]
                          + _dot(p.astype(BF16), v1))
            m_scr[g] = m_new

    @pl.when(j < n_main)
    def _():
        step(k_ref[...], v_ref[...])

    if n_steps > n_main:
        @pl.when(j == n_main)
        def _():
            step(kc_ref[...], vc_ref[...])

    @pl.when(j == n_steps - 1)
    def _():
        for g in range(group):
            acc = acc_scr[g]
            l = acc[:, d:] + jnp.exp2(sink_ref[kvh * group + g] - m_scr[g])
            o_ref[:, g * d:(g + 1) * d] = (acc[:, :d] / l).astype(o_ref.dtype)


def _flash_attention(q_arr, q_col0, k_arr, k_col0, v_arr, v_col0, kc_arr, kc_col0, vc_arr,
                     vc_col0, sink_log2, n_heads, group, scale, tq, tk, with_ctx):
    s_len = q_arr.shape[0]
    kv_len = k_arr.shape[0]
    c_len = kc_arr.shape[0]
    assert q_col0 % group == 0
    n_main = kv_len // tk
    n_steps = n_main + (1 if with_ctx else 0)
    last = n_main - 1
    qw = group * HEAD_DIM
    in_specs = [
        pl.BlockSpec(memory_space=pltpu.SMEM),
        pl.BlockSpec((tq, qw), lambda h, i, j: (i, q_col0 // group + h)),
        pl.BlockSpec((tk, HEAD_DIM), lambda h, i, j: (jnp.minimum(j, last), k_col0 + h)),
        pl.BlockSpec((tk, HEAD_DIM), lambda h, i, j: (jnp.minimum(j, last), v_col0 + h)),
        pl.BlockSpec((c_len, HEAD_DIM), lambda h, i, j: (0, kc_col0 + h)),
        pl.BlockSpec((c_len, HEAD_DIM), lambda h, i, j: (0, vc_col0 + h)),
    ]
    return pl.pallas_call(
        functools.partial(_flash_kernel, n_main=n_main, n_steps=n_steps, scale=scale, group=group),
        out_shape=jax.ShapeDtypeStruct((s_len, n_heads * HEAD_DIM), BF16),
        grid=(n_heads // group, s_len // tq, n_steps),
        in_specs=in_specs,
        out_specs=pl.BlockSpec((tq, qw), lambda h, i, j: (i, h)),
        scratch_shapes=[pltpu.VMEM((group, tq, HEAD_DIM), F32),
                        pltpu.VMEM((group, tq, 2 * HEAD_DIM), F32)],
        compiler_params=_cparams(("parallel", "parallel", "arbitrary")),
        name="flash_attention",
    )(sink_log2, q_arr, k_arr, v_arr, kc_arr, vc_arr)


def _merge_kernel(oa_ref, ob_ref, oc_ref, g0_ref, g1_ref, g2_ref, wa_ref, wb_ref, wc_ref,
                  wo_ref, x_ref, gate_ref, o_ref):
    def sig(g_ref):
        g = g_ref[...].astype(F32)
        return 1.0 / (1.0 + jnp.exp(-g))

    m = (sig(g0_ref) * _dot(oa_ref[...], wa_ref[...])
         + sig(g1_ref) * _dot(ob_ref[...], wb_ref[...])
         + sig(g2_ref) * _dot(oc_ref[...], wc_ref[...]))
    y = _dot(m.astype(BF16), wo_ref[...])
    o_ref[...] = x_ref[...] + gate_ref[...] * y


def _merge(oa, ob, oc, proj, wa, wb, wc, wo, x, gate, tm):
    r, d = x.shape

    def rows(w):
        return pl.BlockSpec((tm, w), lambda i: (i, 0))

    def full(a):
        return pl.BlockSpec(a.shape, lambda i: (0, 0))

    gates = [pl.BlockSpec((tm, d), functools.partial(lambda i, b: (i, b), b=t)) for t in range(3)]
    return pl.pallas_call(
        _merge_kernel,
        out_shape=jax.ShapeDtypeStruct((r, d), F32),
        grid=(r // tm,),
        in_specs=[rows(oa.shape[1]), rows(ob.shape[1]), rows(oc.shape[1])] + gates
        + [full(wa), full(wb), full(wc), full(wo), rows(d), pl.BlockSpec((1, d), lambda i: (0, 0))],
        out_specs=rows(d),
        compiler_params=_cparams(("parallel",)),
        name="merge_branches",
    )(oa, ob, oc, proj, proj, proj, wa, wb, wc, wo, x, gate)


def _topk_ranks(s):
    n_rows = s.shape[0]
    prio = (n_rows - lax.broadcasted_iota(jnp.int32, s.shape, 0)).astype(F32)
    rank = jnp.full(s.shape, float(PEER_TOPK), F32)
    vals = []
    for r in range(PEER_TOPK):
        m = jnp.max(s, axis=0, keepdims=True)
        tied = jnp.where(s == m, prio, 0.0)
        sel = tied == jnp.max(tied, axis=0, keepdims=True)
        rank = jnp.where(sel, float(r), rank)
        s = jnp.where(sel, -jnp.inf, s)
        vals.append(m)
    return rank, vals


def _peer_select_kernel(x_ref, g_ref, sh_ref, sc_ref, wq_ref, keys_ref,
                        ht_ref, n1_ref, a_ref, rk2_ref, b_ref):
    h2 = _norm_mod(x_ref[...], g_ref[...], sh_ref[...], sc_ref[...])
    h2t = h2.T.astype(BF16)
    ht_ref[...] = h2t
    qt = _dot(wq_ref[...], h2t)
    k = PEER_TOPK
    for h in range(PEER_HEADS):
        sv = []
        for p in range(2):
            hp = 2 * h + p
            sv.append(jnp.dot(keys_ref[hp], qt[hp * PEER_NKEYS:(hp + 1) * PEER_NKEYS, :],
                              preferred_element_type=F32, precision=lax.Precision.HIGHEST))
        rank1, d1 = _topk_ranks(sv[0])
        rank2, d2 = _topk_ranks(sv[1])
        hk = k // 2
        d2_all = jnp.concatenate(d2, axis=0)
        d2_low = d2_all[:hk]
        cand = jnp.concatenate([d1[0] + d2_all] + [d1[a] + d2_low for a in range(1, hk)]
                               + [jnp.concatenate(d1[hk:], axis=0) + d2[0]], axis=0)
        rank_c, _ = _topk_ranks(cand)
        chosen = rank_c < float(k)
        c_max = d1[0] + d2[0]
        z = jnp.sum(jnp.where(chosen, jnp.exp(cand - c_max), 0.0), axis=0, keepdims=True)
        picked = jnp.where(chosen, 1.0, 0.0)
        counts = [jnp.sum(picked[0:k], axis=0, keepdims=True)]
        counts += [jnp.sum(picked[k + hk * (a - 1):k + hk * a], axis=0, keepdims=True)
                   for a in range(1, hk)]
        counts += [picked[k + hk * (hk - 1) + a:k + hk * (hk - 1) + a + 1] for a in range(hk)]
        n1 = jnp.zeros(rank1.shape, F32)
        for a in range(k):
            n1 = n1 + jnp.where(rank1 == float(a), counts[a], 0.0)
        n1_ref[h] = n1
        a_ref[h] = jnp.exp(sv[0] - d1[0]) / z
        rk2_ref[h] = rank2.astype(BF16)
        b_ref[h] = jnp.exp(sv[1] - d2[0]).astype(BF16)


def _peer_select(x, g, sh, sc, wq_t, keys, tm):
    r, d = x.shape
    vec = pl.BlockSpec((1, d), lambda i: (0, 0))
    tab = jax.ShapeDtypeStruct((PEER_HEADS, PEER_NKEYS, r), F32)
    tab16 = jax.ShapeDtypeStruct((PEER_HEADS, PEER_NKEYS, r), BF16)
    tab_spec = pl.BlockSpec((PEER_HEADS, PEER_NKEYS, tm), lambda i: (0, 0, i))
    return pl.pallas_call(
        _peer_select_kernel,
        out_shape=(jax.ShapeDtypeStruct((d, r), BF16), tab, tab, tab16, tab16),
        grid=(r // tm,),
        in_specs=[pl.BlockSpec((tm, d), lambda i: (i, 0)), vec, vec, vec,
                  pl.BlockSpec(wq_t.shape, lambda i: (0, 0)),
                  pl.BlockSpec(keys.shape, lambda i: (0, 0, 0))],
        out_specs=(pl.BlockSpec((d, tm), lambda i: (0, i)), tab_spec, tab_spec, tab_spec, tab_spec),
        compiler_params=_cparams(("parallel",)),
        name="peer_select",
    )(x, g, sh, sc, wq_t, keys)


def _peer_expert_kernel(ht_ref, u_ref, vt_ref, n1_ref, a_ref, rk2_ref, b_ref, x_ref, gate_ref,
                        o_ref, acc_scr, c_scr, *, n_chunk):
    kk = pl.program_id(1)

    @pl.when(kk == 0)
    def _():
        acc_scr[...] = jnp.zeros(acc_scr.shape, F32)

    tm = ht_ref.shape[1]
    nk = PEER_NKEYS
    per_stage = PEER_STAGE // nk
    n_stage = n_chunk // per_stage

    def act_dot(s):
        return _dot(u_ref[s * PEER_STAGE:(s + 1) * PEER_STAGE, :], ht_ref[...])

    act_next = act_dot(0)
    for s in range(n_stage):
        act_t = act_next
        if s + 1 < n_stage:
            act_next = act_dot(s + 1)
        for cl in range(per_stage):
            c = s * per_stage + cl
            for l0 in range(0, tm, LANE):
                lanes = slice(l0, l0 + LANE)
                coef = None
                for h in range(PEER_HEADS):
                    n = jnp.broadcast_to(n1_ref[h, c:c + 1, lanes], (nk, LANE)).astype(BF16)
                    a = jnp.broadcast_to(a_ref[h, c:c + 1, lanes], (nk, LANE)).astype(BF16)
                    gate = jnp.minimum(jnp.maximum(n - rk2_ref[h, :, lanes], 0.0), a)
                    term = gate * b_ref[h, :, lanes]
                    coef = term if coef is None else coef + term
                act = act_t[cl * nk:(cl + 1) * nk, lanes]
                c_scr[c * nk:(c + 1) * nk, lanes] = jax.nn.gelu(act).astype(BF16) * coef
        rows = slice(s * PEER_STAGE, (s + 1) * PEER_STAGE)
        acc_scr[...] += _dot(vt_ref[:, rows], c_scr[rows, :])

    @pl.when(kk == pl.num_programs(1) - 1)
    def _():
        o_ref[...] = x_ref[...] + gate_ref[...] * acc_scr[...].T


def _peer_experts(h_t, n1, a, rk2, b, u, v_t, x, gate, tm, ec):
    d, r = h_t.shape
    e = u.shape[0]
    n_chunk = ec // PEER_NKEYS
    assert n_chunk % 8 == 0 and ec % PEER_STAGE == 0
    tab_spec = pl.BlockSpec((PEER_HEADS, PEER_NKEYS, tm), lambda i, k: (0, 0, i))
    row_spec = pl.BlockSpec((PEER_HEADS, n_chunk, tm), lambda i, k: (0, k, i))
    return pl.pallas_call(
        functools.partial(_peer_expert_kernel, n_chunk=n_chunk),
        out_shape=jax.ShapeDtypeStruct((r, d), F32),
        grid=(r // tm, e // ec),
        in_specs=[pl.BlockSpec((d, tm), lambda i, k: (0, i)),
                  pl.BlockSpec((ec, d), lambda i, k: (k, 0)),
                  pl.BlockSpec((d, ec), lambda i, k: (0, k)),
                  row_spec, row_spec, tab_spec, tab_spec,
                  pl.BlockSpec((tm, d), lambda i, k: (i, 0)),
                  pl.BlockSpec((1, d), lambda i, k: (0, 0))],
        out_specs=pl.BlockSpec((tm, d), lambda i, k: (i, 0)),
        scratch_shapes=[pltpu.VMEM((d, tm), F32), pltpu.VMEM((ec, tm), BF16)],
        compiler_params=_cparams(("parallel", "arbitrary")),
        name="peer_experts",
    )(h_t, u, v_t, n1, a, rk2, b, x, gate)


def _final_kernel(x_ref, g_ref, o_ref):
    x = x_ref[...]
    ms = jnp.mean(x * x, axis=-1, keepdims=True)
    o_ref[...] = x * lax.rsqrt(ms + NORM_EPS) * g_ref[...]


def _final_norm(x, g, tm):
    r, d = x.shape
    return pl.pallas_call(
        _final_kernel,
        out_shape=jax.ShapeDtypeStruct((r, d), F32),
        grid=(r // tm,),
        in_specs=[pl.BlockSpec((tm, d), lambda i: (i, 0)), pl.BlockSpec((1, d), lambda i: (0, 0))],
        out_specs=pl.BlockSpec((tm, d), lambda i: (i, 0)),
        compiler_params=_cparams(("parallel",)),
        name="final_norm",
    )(x, g)


def _rope_tables(s_len):
    quarter = HEAD_DIM // 4
    t = jnp.arange(s_len)
    pos_row = (t // GRID_W).astype(F32)
    pos_col = (t % GRID_W).astype(F32)
    inv_freq = ROPE_THETA ** (-jnp.arange(quarter, dtype=F32) / quarter)
    ar = pos_row[:, None] * inv_freq[None, :]
    ac = pos_col[:, None] * inv_freq[None, :]
    cos = jnp.concatenate([jnp.cos(ar), jnp.cos(ar), jnp.cos(ac), jnp.cos(ac)], axis=1)
    sin = jnp.concatenate([-jnp.sin(ar), jnp.sin(ar), -jnp.sin(ac), jnp.sin(ac)], axis=1)
    return cos, sin


def _local_geometry(rows):
    b = LOCAL_BLOCK
    rows_per_block = b // GRID_W
    kh = min(NA_KH, rows)
    q = np.arange(b)
    k = np.arange(3 * b)
    q_row_l, q_col = q // GRID_W, q % GRID_W
    k_row_l, k_col = k // GRID_W - rows_per_block, k % GRID_W
    nb = rows // rows_per_block
    geo = []
    for block in (0, 1, nb - 1):
        q_row = block * rows_per_block + q_row_l
        k_row = block * rows_per_block + k_row_l
        geo.append((q_row, q_col, k_row, k_col))
    return kh, geo


def _na_bias(rpb, rows):
    kh, geo = _local_geometry(rows)
    rpb_rows = LOCAL_BLOCK // GRID_W
    qr, kr = np.arange(rpb_rows), np.arange(3 * rpb_rows) - rpb_rows
    row_sel = (kr[None, :, None] - qr[:, None, None] + (NA_KH - 1)
               == np.arange(2 * NA_KH - 1)[None, None, :]).astype(np.float32)
    col = np.arange(GRID_W)
    col_sel = (col[None, :, None] - col[:, None, None] + (NA_KW - 1)
               == np.arange(2 * NA_KW - 1)[None, None, :]).astype(np.float32)
    hi = lax.Precision.HIGHEST
    t1 = jnp.einsum('hab,cdb->hacd', rpb.astype(F32), col_sel, precision=hi)
    bias = jnp.einsum('hacd,qka->hqckd', t1, row_sel, precision=hi)
    bias = bias.reshape(rpb.shape[0], LOCAL_BLOCK, 3 * LOCAL_BLOCK)
    out = []
    for q_row, q_col, k_row, k_col in geo:
        rs = np.clip(q_row - kh // 2, 0, rows - kh)
        cs = np.clip(q_col - NA_KW // 2, 0, GRID_W - NA_KW)
        ok = ((k_row[None, :] >= rs[:, None]) & (k_row[None, :] < rs[:, None] + kh)
              & (k_col[None, :] >= cs[:, None]) & (k_col[None, :] < cs[:, None] + NA_KW)
              & (k_row[None, :] >= 0) & (k_row[None, :] < rows))
        out.append(jnp.where(jnp.asarray(ok)[None], bias, NEG))
    return jnp.stack(out, axis=0)


def _window_mask(rows):
    _, geo = _local_geometry(rows)
    out = []
    for q_row, q_col, k_row, k_col in geo:
        qp = q_row * GRID_W + q_col
        kp = k_row * GRID_W + k_col
        ok = (np.abs(kp[None, :] - qp[:, None]) <= WINDOW) & (kp[None, :] >= 0) \
            & (kp[None, :] < rows * GRID_W)
        out.append(np.where(ok, 0.0, NEG)[None])
    return jnp.asarray(np.stack(out, axis=0), F32)


def _row_tile(r, pref):
    return pref if r % pref == 0 else LOCAL_BLOCK


def kernel(x, c, ctx, c_ctx, w_mod, b_mod, norm1_g, norm2_g, w_in, rpb_a, qnorm_b, knorm_b, sink_c,
           w_br_a, w_br_b, w_br_c, w_out, peer_wq, peer_keys, peer_u, peer_v, final_g):
    _, s_len, d = x.shape
    c_len = ctx.shape[1]
    depth = w_mod.shape[0]
    rows = s_len // GRID_W
    xs = x[0]
    xc = ctx[0]

    cos, sin = _rope_tables(s_len)
    cos_c = jnp.ones((c_len, HEAD_DIM), F32)
    sin_c = jnp.zeros((c_len, HEAD_DIM), F32)
    win_mask = _window_mask(rows)
    no_sink_a = jnp.full((NA_HEADS,), NEG, F32)
    no_sink_b = jnp.full((GA_HEADS,), NEG, F32)
    c8 = jnp.zeros((8, d), F32).at[0].set(c_ctx).at[1].set(c[0])

    g_blk = 3 * d // HEAD_DIM
    qa0, ka0, va0 = g_blk, g_blk + NA_HEADS, g_blk + 2 * NA_HEADS
    bc0 = g_blk + 3 * NA_HEADS
    vb0 = bc0 + GA_HEADS + GA_KV_HEADS
    vc0 = vb0 + GA_KV_HEADS + WA_HEADS + WA_KV_HEADS
    n_qkv = 3 * A_W + B_QW + 2 * B_KW + C_QW + 2 * C_KW

    tm_s = _row_tile(s_len, 512)
    gb = GA_HEADS // GA_KV_HEADS
    gc = WA_HEADS // WA_KV_HEADS
    for l in range(depth):
        ctx_out = l < depth - 1
        w_in_l = jnp.concatenate([w_in[l][:, n_qkv:], w_in[l][:, :n_qkv]], axis=1).astype(BF16)
        mod = _mod_vectors(c8, w_mod[l], b_mod[l])
        mc = [mod[0:1, t * d:(t + 1) * d] for t in range(N_MOD)]
        ms = [mod[1:2, t * d:(t + 1) * d] for t in range(N_MOD)]
        g1 = norm1_g[l].reshape(1, d)
        g2 = norm2_g[l].reshape(1, d)

        proj = _norm_proj(xs, g1, ms[0], ms[1], w_in_l, tm_s, 1536)
        proj_c = _norm_proj(xc, g1, mc[0], mc[1], w_in_l, c_len, 1536)
        qn = qnorm_b[l].reshape(1, HEAD_DIM)
        kn = knorm_b[l].reshape(1, HEAD_DIM)
        bc_block0 = bc0 // 6
        rope = _rope_heads(proj, cos, sin, qn, kn, tm_s, bc_block0)
        rope_c = _rope_heads(proj_c, cos_c, sin_c, qn, kn, c_len, bc_block0)

        na_scale = HEAD_DIM ** -0.5
        o_a = _local_attention(proj, qa0, proj, ka0, proj, va0, proj_c, ka0, proj_c, va0,
                               _na_bias(rpb_a[l], rows), no_sink_a, NA_HEADS, 1, na_scale)
        o_b = _flash_attention(rope, RO_QB, rope, RO_KB, proj, vb0, rope_c, RO_KB, proj_c, vb0,
                               no_sink_b, GA_HEADS, gb, 1.0, tm_s, _row_tile(s_len, 1024), True)
        o_c = _local_attention(rope, RO_QC, rope, RO_KC, proj, vc0, rope_c, RO_KC, proj_c, vc0,
                               win_mask, sink_c[l], WA_HEADS, gc, 1.0)
        wa = w_br_a[l].astype(BF16)
        wb = w_br_b[l].astype(BF16)
        wc = w_br_c[l].astype(BF16)
        wo = w_out[l].astype(BF16)
        x_mid = _merge(o_a, o_b, o_c, proj, wa, wb, wc, wo, xs, ms[2], LOCAL_BLOCK)

        wq_t = peer_wq[l].T.astype(BF16)
        keys = peer_keys[l].reshape(2 * PEER_HEADS, PEER_NKEYS, -1)
        u = peer_u[l].astype(BF16)
        v_t = peer_v[l].T.astype(BF16)
        sel = _peer_select(x_mid, g2, ms[3], ms[4], wq_t, keys, LOCAL_BLOCK)
        xs = _peer_experts(*sel, u=u, v_t=v_t, x=x_mid, gate=ms[5], tm=tm_s,
                           ec=PEER_EXPERT_CHUNK)

        if ctx_out:
            oc_a = _flash_attention(proj_c, qa0, proj_c, ka0, proj_c, va0, proj_c, ka0, proj_c,
                                    va0, no_sink_a, NA_HEADS, 1, na_scale * LOG2E, c_len, c_len,
                                    False)
            oc_b = _flash_attention(rope_c, RO_QB, rope_c, RO_KB, proj_c, vb0, rope_c, RO_KB,
                                    proj_c, vb0, no_sink_b, GA_HEADS, gb, 1.0, c_len, c_len, False)
            oc_c = _flash_attention(rope_c, RO_QC, rope_c, RO_KC, proj_c, vc0, rope_c, RO_KC,
                                    proj_c, vc0, sink_c[l] * LOG2E, WA_HEADS, gc, LOG2E, c_len,
                                    c_len, False)
            xc_mid = _merge(oc_a, oc_b, oc_c, proj_c, wa, wb, wc, wo, xc, mc[2], c_len)
            sel_c = _peer_select(xc_mid, g2, mc[3], mc[4], wq_t, keys, c_len)
            xc = _peer_experts(*sel_c, u=u, v_t=v_t, x=xc_mid, gate=mc[5], tm=c_len,
                               ec=PEER_EXPERT_CHUNK)

    return _final_norm(xs, final_g.reshape(1, d), tm_s).reshape(1, s_len, d)
```

```python
import functools

import jax
import jax.numpy as jnp
import numpy as np
from jax import lax
from jax.experimental import pallas as pl
from jax.experimental.pallas import tpu as pltpu

F32 = jnp.float32
BF16 = jnp.bfloat16

GRID_W = 64
HEAD_DIM = 128
NA_HEADS = 6
NA_KH = 8
NA_KW = 16
GA_HEADS = 6
GA_KV_HEADS = 2
WA_HEADS = 4
WA_KV_HEADS = 2
WINDOW = 128
ROPE_THETA = 10000.0
NORM_EPS = 1e-6
NEG = -1e30
LOG2E = 1.4426950408889634
PEER_HEADS = 8
PEER_NKEYS = 128
PEER_TOPK = 16
N_MOD = 6

A_W = NA_HEADS * HEAD_DIM
B_QW = GA_HEADS * HEAD_DIM
B_KW = GA_KV_HEADS * HEAD_DIM
C_QW = WA_HEADS * HEAD_DIM
C_KW = WA_KV_HEADS * HEAD_DIM

V7X_VMEM_BYTES = 64 * 1024 * 1024
VMEM_LIMIT = V7X_VMEM_BYTES - 8 * 1024 * 1024
LANE = 128
BF16_ROWS = 16

LOCAL_BLOCK = 256
PEER_EXPERT_CHUNK = 512


def _cparams(sem, flags=None):
    return pltpu.CompilerParams(dimension_semantics=sem, vmem_limit_bytes=VMEM_LIMIT, flags=flags)


def _dot(a, b):
    return jnp.dot(a, b, preferred_element_type=F32)


def _dot_nt(a, b):
    return lax.dot_general(a, b, (((1,), (1,)), ((), ())), preferred_element_type=F32)


def _norm_mod(x, g, sh, sc):
    ms = jnp.mean(x * x, axis=-1, keepdims=True)
    y = x * lax.rsqrt(ms + NORM_EPS) * g
    return y * (1.0 + sc) + sh


def _mod_kernel(c_ref, w_ref, b_ref, o_ref):
    c = c_ref[...]
    a = c / (1.0 + jnp.exp(-c))
    o_ref[...] = jnp.dot(a, w_ref[...], preferred_element_type=F32,
                         precision=lax.Precision.HIGHEST) + b_ref[...]


def _mod_vectors(c8, w_mod, b_mod):
    d, n = w_mod.shape
    tn = 1024
    return pl.pallas_call(
        _mod_kernel,
        out_shape=jax.ShapeDtypeStruct((8, n), F32),
        grid=(n // tn,),
        in_specs=[pl.BlockSpec((8, d), lambda j: (0, 0)),
                  pl.BlockSpec((d, tn), lambda j: (0, j)),
                  pl.BlockSpec((1, tn), lambda j: (0, j))],
        out_specs=pl.BlockSpec((8, tn), lambda j: (0, j)),
        compiler_params=_cparams(("arbitrary",)),
        name="mod_vectors",
    )(c8, w_mod, b_mod.reshape(1, n))


def _proj_kernel(x_ref, g_ref, sh_ref, sc_ref, w_ref, o_ref, h_scr):
    @pl.when(pl.program_id(1) == 0)
    def _():
        h_scr[...] = _norm_mod(x_ref[...], g_ref[...], sh_ref[...], sc_ref[...]).astype(BF16)

    o_ref[...] = _dot(h_scr[...], w_ref[...]).astype(o_ref.dtype)


def _norm_proj(x, g, sh, sc, w, tm, tn):
    r, d = x.shape
    n = w.shape[1]
    vec = pl.BlockSpec((1, d), lambda i, j: (0, 0))
    return pl.pallas_call(
        _proj_kernel,
        out_shape=jax.ShapeDtypeStruct((r, n), BF16),
        grid=(r // tm, n // tn),
        in_specs=[pl.BlockSpec((tm, d), lambda i, j: (i, 0)), vec, vec, vec,
                  pl.BlockSpec((d, tn), lambda i, j: (0, j))],
        out_specs=pl.BlockSpec((tm, tn), lambda i, j: (i, j)),
        scratch_shapes=[pltpu.VMEM((tm, d), BF16)],
        compiler_params=_cparams(("parallel", "arbitrary")),
        name="norm_proj",
    )(x, g, sh, sc, w)


def _rope_kernel(p0_ref, p1_ref, p2_ref, cos_ref, sin_ref, qn_ref, kn_ref, o_ref):
    cos = cos_ref[...]
    sin = sin_ref[...]
    lane = lax.broadcasted_iota(jnp.int32, cos.shape, 1)
    low = (lane % (HEAD_DIM // 2)) < (HEAD_DIM // 4)
    scale = HEAD_DIM ** -0.5
    src = jnp.concatenate([p0_ref[...], p1_ref[...], p2_ref[...]], axis=1).astype(F32)

    def head(col):
        return src[:, col * HEAD_DIM:(col + 1) * HEAD_DIM]

    def rms(x, g):
        ms = jnp.mean(x * x, axis=-1, keepdims=True)
        return x * lax.rsqrt(ms + NORM_EPS) * g

    def rope(x):
        partner = jnp.where(low, pltpu.roll(x, HEAD_DIM - HEAD_DIM // 4, 1),
                            pltpu.roll(x, HEAD_DIM // 4, 1))
        return x * cos + partner * sin

    outs = []
    for hq in range(GA_HEADS):
        outs.append(rope(rms(head(hq), qn_ref[...])) * (scale * LOG2E))
    for hk in range(GA_KV_HEADS):
        outs.append(rope(rms(head(GA_HEADS + hk), kn_ref[...])))
    c0 = GA_HEADS + 2 * GA_KV_HEADS
    for hq in range(WA_HEADS):
        outs.append(rope(head(c0 + hq)) * scale)
    for hk in range(WA_KV_HEADS):
        outs.append(rope(head(c0 + WA_HEADS + hk)))
    o_ref[...] = jnp.concatenate(outs, axis=1).astype(o_ref.dtype)


ROPE_OUT_HEADS = GA_HEADS + GA_KV_HEADS + WA_HEADS + WA_KV_HEADS
RO_QB, RO_KB, RO_QC, RO_KC = 0, GA_HEADS, GA_HEADS + GA_KV_HEADS, GA_HEADS + GA_KV_HEADS + WA_HEADS


def _rope_heads(proj, cos, sin, qn, kn, tm, bc_block0):
    r = proj.shape[0]
    wblk = 6 * HEAD_DIM
    pin = [pl.BlockSpec((tm, wblk), functools.partial(lambda i, b: (i, b), b=bc_block0 + t))
           for t in range(3)]
    tab = pl.BlockSpec((tm, HEAD_DIM), lambda i: (i, 0))
    vec = pl.BlockSpec((1, HEAD_DIM), lambda i: (0, 0))
    return pl.pallas_call(
        _rope_kernel,
        out_shape=jax.ShapeDtypeStruct((r, ROPE_OUT_HEADS * HEAD_DIM), BF16),
        grid=(r // tm,),
        in_specs=pin + [tab, tab, vec, vec],
        out_specs=pl.BlockSpec((tm, ROPE_OUT_HEADS * HEAD_DIM), lambda i: (i, 0)),
        compiler_params=_cparams(("parallel",)),
        name="qk_norm_rope",
    )(proj, proj, proj, cos, sin, qn, kn)


def _local_kernel(sink_ref, q_ref, k0_ref, k1_ref, k2_ref, v0_ref, v1_ref, v2_ref,
                  kc_ref, vc_ref, bias_ref, o_ref, *, scale):
    h = pl.program_id(0)
    q = q_ref[...]
    s = jnp.concatenate([_dot_nt(q, k0_ref[...]), _dot_nt(q, k1_ref[...]),
                         _dot_nt(q, k2_ref[...])], axis=1)
    sc = _dot_nt(q, kc_ref[...])
    if scale != 1.0:
        s = s * scale
        sc = sc * scale
    s = s + bias_ref[0, 0]
    m = jnp.maximum(jnp.max(s, axis=1, keepdims=True), jnp.max(sc, axis=1, keepdims=True))
    p = jnp.exp(s - m)
    pc = jnp.exp(sc - m)
    l = (jnp.sum(p, axis=1, keepdims=True) + jnp.sum(pc, axis=1, keepdims=True)
         + jnp.exp(sink_ref[h] - m))
    b = LOCAL_BLOCK
    pb = p.astype(BF16)
    o = (_dot(pb[:, 0:b], v0_ref[...]) + _dot(pb[:, b:2 * b], v1_ref[...])
         + _dot(pb[:, 2 * b:3 * b], v2_ref[...]) + _dot(pc.astype(BF16), vc_ref[...]))
    o_ref[...] = (o / l).astype(o_ref.dtype)


def _local_attention(q_arr, q_col0, k_arr, k_col0, v_arr, v_col0, kc_arr, kc_col0, vc_arr,
                     vc_col0, bias, sink, n_heads, group, scale):
    s_len = q_arr.shape[0]
    c_len = kc_arr.shape[0]
    b = LOCAL_BLOCK
    nb = s_len // b
    hb = bias.shape[1]

    def kv_spec(col0, off):
        return pl.BlockSpec(
            (b, HEAD_DIM), lambda h, i: (jnp.clip(i + off, 0, nb - 1), col0 + h // group))

    def variant(i):
        return jnp.where(i == 0, 0, jnp.where(i == nb - 1, 2, 1))

    in_specs = [
        pl.BlockSpec(memory_space=pltpu.SMEM),
        pl.BlockSpec((b, HEAD_DIM), lambda h, i: (i, q_col0 + h)),
        kv_spec(k_col0, -1), kv_spec(k_col0, 0), kv_spec(k_col0, 1),
        kv_spec(v_col0, -1), kv_spec(v_col0, 0), kv_spec(v_col0, 1),
        pl.BlockSpec((c_len, HEAD_DIM), lambda h, i: (0, kc_col0 + h // group)),
        pl.BlockSpec((c_len, HEAD_DIM), lambda h, i: (0, vc_col0 + h // group)),
        pl.BlockSpec((1, 1, b, 3 * b),
                     lambda h, i: (variant(i), h if hb > 1 else 0, 0, 0)),
    ]
    return pl.pallas_call(
        functools.partial(_local_kernel, scale=scale),
        out_shape=jax.ShapeDtypeStruct((s_len, n_heads * HEAD_DIM), BF16),
        grid=(n_heads, nb),
        in_specs=in_specs,
        out_specs=pl.BlockSpec((b, HEAD_DIM), lambda h, i: (i, h)),
        compiler_params=_cparams(("parallel", "arbitrary")),
        name="local_attention",
    )(sink, q_arr, k_arr, k_arr, k_arr, v_arr, v_arr, v_arr, kc_arr, vc_arr, bias)


def _flash_kernel(sink_ref, q_ref, k_ref, v_ref, kc_ref, vc_ref, o_ref, m_scr, acc_scr,
                  *, n_main, n_steps, scale, group):
    kvh = pl.program_id(0)
    j = pl.program_id(2)
    d = HEAD_DIM

    @pl.when(j == 0)
    def _():
        m_scr[...] = jnp.full(m_scr.shape, NEG, F32)
        acc_scr[...] = jnp.zeros(acc_scr.shape, F32)

    def step(k, v):
        tk = k.shape[0]
        v1 = jnp.concatenate([v, jnp.ones((tk, d), BF16)], axis=1)
        for g in range(group):
            s = _dot_nt(q_ref[:, g * d:(g + 1) * d], k)
            if scale != 1.0:
                s = s * scale
            mx = s[:, 0:d]
            for t in range(1, tk // d):
                mx = jnp.maximum(mx, s[:, t * d:(t + 1) * d])
            m_prev = m_scr[g]
            m_new = jnp.maximum(m_prev, jnp.max(mx, axis=1, keepdims=True))
            alpha = jnp.exp2(m_prev - m_new)
            p = jnp.exp2(s - jnp.concatenate([m_new] * (tk // d), axis=1))
            acc_scr[g] = (jnp.concatenate([alpha, alpha], axis=1) * acc_scr[g]
                          + _dot(p.astype(BF16), v1))
            m_scr[g] = m_new

    @pl.when(j < n_main)
    def _():
        step(k_ref[...], v_ref[...])

    if n_steps > n_main:
        @pl.when(j == n_main)
        def _():
            step(kc_ref[...], vc_ref[...])

    @pl.when(j == n_steps - 1)
    def _():
        for g in range(group):
            acc = acc_scr[g]
            l = acc[:, d:] + jnp.exp2(sink_ref[kvh * group + g] - m_scr[g])
            o_ref[:, g * d:(g + 1) * d] = (acc[:, :d] / l).astype(o_ref.dtype)


def _flash_attention(q_arr, q_col0, k_arr, k_col0, v_arr, v_col0, kc_arr, kc_col0, vc_arr,
                     vc_col0, sink_log2, n_heads, group, scale, tq, tk, with_ctx):
    s_len = q_arr.shape[0]
    kv_len = k_arr.shape[0]
    c_len = kc_arr.shape[0]
    assert q_col0 % group == 0
    n_main = kv_len // tk
    n_steps = n_main + (1 if with_ctx else 0)
    last = n_main - 1
    qw = group * HEAD_DIM
    in_specs = [
        pl.BlockSpec(memory_space=pltpu.SMEM),
        pl.BlockSpec((tq, qw), lambda h, i, j: (i, q_col0 // group + h)),
        pl.BlockSpec((tk, HEAD_DIM), lambda h, i, j: (jnp.minimum(j, last), k_col0 + h)),
        pl.BlockSpec((tk, HEAD_DIM), lambda h, i, j: (jnp.minimum(j, last), v_col0 + h)),
        pl.BlockSpec((c_len, HEAD_DIM), lambda h, i, j: (0, kc_col0 + h)),
        pl.BlockSpec((c_len, HEAD_DIM), lambda h, i, j: (0, vc_col0 + h)),
    ]
    return pl.pallas_call(
        functools.partial(_flash_kernel, n_main=n_main, n_steps=n_steps, scale=scale, group=group),
        out_shape=jax.ShapeDtypeStruct((s_len, n_heads * HEAD_DIM), BF16),
        grid=(n_heads // group, s_len // tq, n_steps),
        in_specs=in_specs,
        out_specs=pl.BlockSpec((tq, qw), lambda h, i, j: (i, h)),
        scratch_shapes=[pltpu.VMEM((group, tq, HEAD_DIM), F32),
                        pltpu.VMEM((group, tq, 2 * HEAD_DIM), F32)],
        compiler_params=_cparams(("parallel", "parallel", "arbitrary")),
        name="flash_attention",
    )(sink_log2, q_arr, k_arr, v_arr, kc_arr, vc_arr)


def _merge_kernel(oa_ref, ob_ref, oc_ref, g0_ref, g1_ref, g2_ref, wa_ref, wb_ref, wc_ref,
                  wo_ref, x_ref, gate_ref, o_ref):
    def sig(g_ref):
        g = g_ref[...].astype(F32)
        return 1.0 / (1.0 + jnp.exp(-g))

    m = (sig(g0_ref) * _dot(oa_ref[...], wa_ref[...])
         + sig(g1_ref) * _dot(ob_ref[...], wb_ref[...])
         + sig(g2_ref) * _dot(oc_ref[...], wc_ref[...]))
    y = _dot(m.astype(BF16), wo_ref[...])
    o_ref[...] = x_ref[...] + gate_ref[...] * y


def _merge(oa, ob, oc, proj, wa, wb, wc, wo, x, gate, tm):
    r, d = x.shape

    def rows(w):
        return pl.BlockSpec((tm, w), lambda i: (i, 0))

    def full(a):
        return pl.BlockSpec(a.shape, lambda i: (0, 0))

    gates = [pl.BlockSpec((tm, d), functools.partial(lambda i, b: (i, b), b=t)) for t in range(3)]
    return pl.pallas_call(
        _merge_kernel,
        out_shape=jax.ShapeDtypeStruct((r, d), F32),
        grid=(r // tm,),
        in_specs=[rows(oa.shape[1]), rows(ob.shape[1]), rows(oc.shape[1])] + gates
        + [full(wa), full(wb), full(wc), full(wo), rows(d), pl.BlockSpec((1, d), lambda i: (0, 0))],
        out_specs=rows(d),
        compiler_params=_cparams(("parallel",)),
        name="merge_branches",
    )(oa, ob, oc, proj, proj, proj, wa, wb, wc, wo, x, gate)


def _topk_ranks(s, rank_ref, vals_ref):
    n_rows = s.shape[0]
    k = PEER_TOPK
    not_top = float(k)

    def extract(one_per_round):
        cur = s
        rank = jnp.full(s.shape, not_top, F32)
        vals = []
        if one_per_round:
            prio = (n_rows - lax.broadcasted_iota(jnp.int32, s.shape, 0)).astype(F32)
        for r in range(k):
            m = jnp.max(cur, axis=0, keepdims=True)
            sel = cur == m
            if one_per_round:
                tied = jnp.where(sel, prio, 0.0)
                sel = tied == jnp.max(tied, axis=0, keepdims=True)
            rank = jnp.where(sel, float(r), rank)
            cur = jnp.where(sel, -jnp.inf, cur)
            vals.append(m)
        rank_ref[0:n_rows, :] = rank
        if vals_ref is not None:
            vals_ref[...] = jnp.concatenate(vals, axis=0)
        return rank

    rank = extract(False)
    removed = jnp.sum(jnp.where(rank < not_top, 1.0, 0.0), axis=0, keepdims=True)

    @pl.when(jnp.max(removed) > not_top)
    def _():
        extract(True)

    return rank_ref[0:n_rows, :], (None if vals_ref is None else vals_ref[...])


def _peer_select_kernel(x_ref, g_ref, sh_ref, sc_ref, wq_ref, keys_ref,
                        ht_ref, n1_ref, a_ref, rk2_ref, b_ref, rank_scr, vals_scr):
    h2 = _norm_mod(x_ref[...], g_ref[...], sh_ref[...], sc_ref[...])
    h2b = h2.astype(BF16)
    ht_ref[...] = h2b
    qt = _dot_nt(wq_ref[...], h2b)
    k = PEER_TOPK
    for h in range(PEER_HEADS):
        sv = []
        for p in range(2):
            hp = 2 * h + p
            sv.append(jnp.dot(keys_ref[hp], qt[hp * PEER_NKEYS:(hp + 1) * PEER_NKEYS, :],
                              preferred_element_type=F32, precision=lax.Precision.HIGHEST))
        rank1, d1 = _topk_ranks(sv[0], rank_scr.at[0], vals_scr.at[0])
        rank2, d2 = _topk_ranks(sv[1], rank_scr.at[1], vals_scr.at[1])
        hk = k // 2
        d2_low = d2[:hk]
        cand = jnp.concatenate([d1[0:1] + d2] + [d1[a:a + 1] + d2_low for a in range(1, hk)]
                               + [d1[hk:] + d2[0:1]], axis=0)
        rank_c, _ = _topk_ranks(cand, rank_scr.at[2], None)
        chosen = rank_c < float(k)
        c_max = d1[0:1] + d2[0:1]
        z = jnp.sum(jnp.where(chosen, jnp.exp(cand - c_max), 0.0), axis=0, keepdims=True)
        picked = jnp.where(chosen, 1.0, 0.0)
        counts = [jnp.sum(picked[0:k], axis=0, keepdims=True)]
        counts += [jnp.sum(picked[k + hk * (a - 1):k + hk * a], axis=0, keepdims=True)
                   for a in range(1, hk)]
        counts += [picked[k + hk * (hk - 1) + a:k + hk * (hk - 1) + a + 1] for a in range(hk)]
        n1 = jnp.zeros(rank1.shape, F32)
        for a in range(k):
            n1 = n1 + jnp.where(rank1 == float(a), counts[a], 0.0)
        n1_ref[h] = n1
        a_ref[h] = jnp.exp(sv[0] - d1[0:1]) / z
        rk2_ref[h] = rank2.astype(BF16)
        b_ref[h] = jnp.exp(sv[1] - d2[0:1]).astype(BF16)


def _peer_select(x, g, sh, sc, wq_t, keys, tm):
    r, d = x.shape
    vec = pl.BlockSpec((1, d), lambda i: (0, 0))
    tab = jax.ShapeDtypeStruct((PEER_HEADS, PEER_NKEYS, r), F32)
    tab16 = jax.ShapeDtypeStruct((PEER_HEADS, PEER_NKEYS, r), BF16)
    tab_spec = pl.BlockSpec((PEER_HEADS, PEER_NKEYS, tm), lambda i: (0, 0, i))
    return pl.pallas_call(
        _peer_select_kernel,
        out_shape=(jax.ShapeDtypeStruct((r, d), BF16), tab, tab, tab16, tab16),
        grid=(r // tm,),
        in_specs=[pl.BlockSpec((tm, d), lambda i: (i, 0)), vec, vec, vec,
                  pl.BlockSpec(wq_t.shape, lambda i: (0, 0)),
                  pl.BlockSpec(keys.shape, lambda i: (0, 0, 0))],
        out_specs=(pl.BlockSpec((tm, d), lambda i: (i, 0)), tab_spec, tab_spec, tab_spec, tab_spec),
        scratch_shapes=[pltpu.VMEM((3, PEER_NKEYS, tm), F32),
                        pltpu.VMEM((2, PEER_TOPK, tm), F32)],
        compiler_params=_cparams(("parallel",)),
        name="peer_select",
    )(x, g, sh, sc, wq_t, keys)


def _peer_expert_kernel(ht_ref, u_ref, vt_ref, n1_ref, a_ref, rk2_ref, b_ref, x_ref, gate_ref,
                        o_ref, acc_scr, act_scr, c_scr, *, n_chunk):
    kk = pl.program_id(1)
    tm = ht_ref.shape[0]
    nk = PEER_NKEYS

    @pl.when(kk == 0)
    def _():
        acc_scr[...] = jnp.zeros(acc_scr.shape, F32)
        act_scr[...] = jnp.zeros(act_scr.shape, F32)
        c_scr[...] = jnp.zeros(c_scr.shape, BF16)

    def step(cur, prev):
        act_scr[cur] = _dot_nt(u_ref[...], ht_ref[...])
        n_tiles = n_chunk * (tm // LANE)
        dm = acc_scr.shape[0] // n_tiles
        for c in range(n_chunk):
            rows = slice(c * nk, (c + 1) * nk)
            for lt in range(tm // LANE):
                lanes = slice(lt * LANE, (lt + 1) * LANE)
                coef = None
                for h in range(PEER_HEADS):
                    n = jnp.broadcast_to(n1_ref[h, 0, c:c + 1, lanes], (nk, LANE)).astype(BF16)
                    a = jnp.broadcast_to(a_ref[h, 0, c:c + 1, lanes], (nk, LANE)).astype(BF16)
                    gate = jnp.minimum(jnp.maximum(n - rk2_ref[h, :, lanes], 0.0), a)
                    term = gate * b_ref[h, :, lanes]
                    coef = term if coef is None else coef + term
                act = act_scr[prev, rows, lanes]
                c_scr[prev, rows, lanes] = jax.nn.gelu(act).astype(BF16) * coef
                tile = c * (tm // LANE) + lt
                out_rows = slice(tile * dm, (tile + 1) * dm)
                acc_scr[out_rows, :] += _dot(vt_ref[out_rows, :], c_scr[cur])

    @pl.when(kk % 2 == 0)
    def _():
        step(0, 1)

    @pl.when(kk % 2 == 1)
    def _():
        step(1, 0)

    @pl.when(kk == pl.num_programs(1) - 1)
    def _():
        o_ref[...] = x_ref[...] + gate_ref[...] * acc_scr[...].T


PEER_PIPE = 2


def _peer_experts(h_t, n1, a, rk2, b, u, v_t, x, gate, tm, ec):
    r, d = h_t.shape
    e = u.shape[0]
    n_chunk = ec // PEER_NKEYS
    nke = e // ec
    last = nke - 1
    n1 = n1.reshape(PEER_HEADS, nke, n_chunk, r)
    a = a.reshape(PEER_HEADS, nke, n_chunk, r)
    tab_spec = pl.BlockSpec((PEER_HEADS, PEER_NKEYS, tm), lambda i, k: (0, 0, i))
    row_spec = pl.BlockSpec((PEER_HEADS, 1, n_chunk, tm),
                            lambda i, k: (0, jnp.clip(k - 1, 0, last), 0, i))
    return pl.pallas_call(
        functools.partial(_peer_expert_kernel, n_chunk=n_chunk),
        out_shape=jax.ShapeDtypeStruct((r, d), F32),
        grid=(r // tm, nke + PEER_PIPE),
        in_specs=[pl.BlockSpec((tm, d), lambda i, k: (i, 0)),
                  pl.BlockSpec((ec, d), lambda i, k: (jnp.minimum(k, last), 0)),
                  pl.BlockSpec((d, ec), lambda i, k: (0, jnp.clip(k - 2, 0, last))),
                  row_spec, row_spec, tab_spec, tab_spec,
                  pl.BlockSpec((tm, d), lambda i, k: (i, 0)),
                  pl.BlockSpec((1, d), lambda i, k: (0, 0))],
        out_specs=pl.BlockSpec((tm, d), lambda i, k: (i, 0)),
        scratch_shapes=[pltpu.VMEM((d, tm), F32), pltpu.VMEM((2, ec, tm), F32),
                        pltpu.VMEM((2, ec, tm), BF16)],
        compiler_params=_cparams(("parallel", "arbitrary")),
        name="peer_experts",
    )(h_t, u, v_t, n1, a, rk2, b, x, gate)


def _final_kernel(x_ref, g_ref, o_ref):
    x = x_ref[...]
    ms = jnp.mean(x * x, axis=-1, keepdims=True)
    o_ref[...] = x * lax.rsqrt(ms + NORM_EPS) * g_ref[...]


def _final_norm(x, g, tm):
    r, d = x.shape
    return pl.pallas_call(
        _final_kernel,
        out_shape=jax.ShapeDtypeStruct((r, d), F32),
        grid=(r // tm,),
        in_specs=[pl.BlockSpec((tm, d), lambda i: (i, 0)), pl.BlockSpec((1, d), lambda i: (0, 0))],
        out_specs=pl.BlockSpec((tm, d), lambda i: (i, 0)),
        compiler_params=_cparams(("parallel",)),
        name="final_norm",
    )(x, g)


def _rope_tables(s_len):
    quarter = HEAD_DIM // 4
    t = jnp.arange(s_len)
    pos_row = (t // GRID_W).astype(F32)
    pos_col = (t % GRID_W).astype(F32)
    inv_freq = ROPE_THETA ** (-jnp.arange(quarter, dtype=F32) / quarter)
    ar = pos_row[:, None] * inv_freq[None, :]
    ac = pos_col[:, None] * inv_freq[None, :]
    cos = jnp.concatenate([jnp.cos(ar), jnp.cos(ar), jnp.cos(ac), jnp.cos(ac)], axis=1)
    sin = jnp.concatenate([-jnp.sin(ar), jnp.sin(ar), -jnp.sin(ac), jnp.sin(ac)], axis=1)
    return cos, sin


def _local_geometry(rows):
    b = LOCAL_BLOCK
    rows_per_block = b // GRID_W
    kh = min(NA_KH, rows)
    q = np.arange(b)
    k = np.arange(3 * b)
    q_row_l, q_col = q // GRID_W, q % GRID_W
    k_row_l, k_col = k // GRID_W - rows_per_block, k % GRID_W
    nb = rows // rows_per_block
    geo = []
    for block in (0, 1, nb - 1):
        q_row = block * rows_per_block + q_row_l
        k_row = block * rows_per_block + k_row_l
        geo.append((q_row, q_col, k_row, k_col))
    return kh, geo


def _na_bias(rpb, rows):
    kh, geo = _local_geometry(rows)
    rpb_rows = LOCAL_BLOCK // GRID_W
    qr, kr = np.arange(rpb_rows), np.arange(3 * rpb_rows) - rpb_rows
    row_sel = (kr[None, :, None] - qr[:, None, None] + (NA_KH - 1)
               == np.arange(2 * NA_KH - 1)[None, None, :]).astype(np.float32)
    col = np.arange(GRID_W)
    col_sel = (col[None, :, None] - col[:, None, None] + (NA_KW - 1)
               == np.arange(2 * NA_KW - 1)[None, None, :]).astype(np.float32)
    hi = lax.Precision.HIGHEST
    t1 = jnp.einsum('hab,cdb->hacd', rpb.astype(F32), col_sel, precision=hi)
    bias = jnp.einsum('hacd,qka->hqckd', t1, row_sel, precision=hi)
    bias = bias.reshape(rpb.shape[0], LOCAL_BLOCK, 3 * LOCAL_BLOCK)
    out = []
    for q_row, q_col, k_row, k_col in geo:
        rs = np.clip(q_row - kh // 2, 0, rows - kh)
        cs = np.clip(q_col - NA_KW // 2, 0, GRID_W - NA_KW)
        ok = ((k_row[None, :] >= rs[:, None]) & (k_row[None, :] < rs[:, None] + kh)
              & (k_col[None, :] >= cs[:, None]) & (k_col[None, :] < cs[:, None] + NA_KW)
              & (k_row[None, :] >= 0) & (k_row[None, :] < rows))
        out.append(jnp.where(jnp.asarray(ok)[None], bias, NEG))
    return jnp.stack(out, axis=0)


def _window_mask(rows):
    _, geo = _local_geometry(rows)
    out = []
    for q_row, q_col, k_row, k_col in geo:
        qp = q_row * GRID_W + q_col
        kp = k_row * GRID_W + k_col
        ok = (np.abs(kp[None, :] - qp[:, None]) <= WINDOW) & (kp[None, :] >= 0) \
            & (kp[None, :] < rows * GRID_W)
        out.append(np.where(ok, 0.0, NEG)[None])
    return jnp.asarray(np.stack(out, axis=0), F32)


def _row_tile(r, pref):
    return pref if r % pref == 0 else LOCAL_BLOCK


def kernel(x, c, ctx, c_ctx, w_mod, b_mod, norm1_g, norm2_g, w_in, rpb_a, qnorm_b, knorm_b, sink_c,
           w_br_a, w_br_b, w_br_c, w_out, peer_wq, peer_keys, peer_u, peer_v, final_g):
    _, s_len, d = x.shape
    c_len = ctx.shape[1]
    depth = w_mod.shape[0]
    rows = s_len // GRID_W
    xs = x[0]
    xc = ctx[0]

    cos, sin = _rope_tables(s_len)
    cos_c = jnp.ones((c_len, HEAD_DIM), F32)
    sin_c = jnp.zeros((c_len, HEAD_DIM), F32)
    win_mask = _window_mask(rows)
    no_sink_a = jnp.full((NA_HEADS,), NEG, F32)
    no_sink_b = jnp.full((GA_HEADS,), NEG, F32)
    c8 = jnp.zeros((8, d), F32).at[0].set(c_ctx).at[1].set(c[0])

    g_blk = 3 * d // HEAD_DIM
    qa0, ka0, va0 = g_blk, g_blk + NA_HEADS, g_blk + 2 * NA_HEADS
    bc0 = g_blk + 3 * NA_HEADS
    vb0 = bc0 + GA_HEADS + GA_KV_HEADS
    vc0 = vb0 + GA_KV_HEADS + WA_HEADS + WA_KV_HEADS
    n_qkv = 3 * A_W + B_QW + 2 * B_KW + C_QW + 2 * C_KW

    tm_s = _row_tile(s_len, 512)
    gb = GA_HEADS // GA_KV_HEADS
    gc = WA_HEADS // WA_KV_HEADS
    for l in range(depth):
        ctx_out = l < depth - 1
        w_in_l = jnp.concatenate([w_in[l][:, n_qkv:], w_in[l][:, :n_qkv]], axis=1).astype(BF16)
        mod = _mod_vectors(c8, w_mod[l], b_mod[l])
        mc = [mod[0:1, t * d:(t + 1) * d] for t in range(N_MOD)]
        ms = [mod[1:2, t * d:(t + 1) * d] for t in range(N_MOD)]
        g1 = norm1_g[l].reshape(1, d)
        g2 = norm2_g[l].reshape(1, d)

        proj = _norm_proj(xs, g1, ms[0], ms[1], w_in_l, tm_s, 1536)
        proj_c = _norm_proj(xc, g1, mc[0], mc[1], w_in_l, c_len, 1536)
        qn = qnorm_b[l].reshape(1, HEAD_DIM)
        kn = knorm_b[l].reshape(1, HEAD_DIM)
        bc_block0 = bc0 // 6
        rope = _rope_heads(proj, cos, sin, qn, kn, tm_s, bc_block0)
        rope_c = _rope_heads(proj_c, cos_c, sin_c, qn, kn, c_len, bc_block0)

        na_scale = HEAD_DIM ** -0.5
        o_a = _local_attention(proj, qa0, proj, ka0, proj, va0, proj_c, ka0, proj_c, va0,
                               _na_bias(rpb_a[l], rows), no_sink_a, NA_HEADS, 1, na_scale)
        o_b = _flash_attention(rope, RO_QB, rope, RO_KB, proj, vb0, rope_c, RO_KB, proj_c, vb0,
                               no_sink_b, GA_HEADS, gb, 1.0, tm_s, _row_tile(s_len, 1024), True)
        o_c = _local_attention(rope, RO_QC, rope, RO_KC, proj, vc0, rope_c, RO_KC, proj_c, vc0,
                               win_mask, sink_c[l], WA_HEADS, gc, 1.0)
        wa = w_br_a[l].astype(BF16)
        wb = w_br_b[l].astype(BF16)
        wc = w_br_c[l].astype(BF16)
        wo = w_out[l].astype(BF16)
        x_mid = _merge(o_a, o_b, o_c, proj, wa, wb, wc, wo, xs, ms[2], LOCAL_BLOCK)

        wq_t = peer_wq[l].T.astype(BF16)
        keys = peer_keys[l].reshape(2 * PEER_HEADS, PEER_NKEYS, -1)
        u = peer_u[l].astype(BF16)
        v_t = peer_v[l].T.astype(BF16)
        sel = _peer_select(x_mid, g2, ms[3], ms[4], wq_t, keys, tm_s)
        xs = _peer_experts(*sel, u=u, v_t=v_t, x=x_mid, gate=ms[5], tm=tm_s,
                           ec=PEER_EXPERT_CHUNK)

        if ctx_out:
            oc_a = _flash_attention(proj_c, qa0, proj_c, ka0, proj_c, va0, proj_c, ka0, proj_c,
                                    va0, no_sink_a, NA_HEADS, 1, na_scale * LOG2E, c_len, c_len,
                                    False)
            oc_b = _flash_attention(rope_c, RO_QB, rope_c, RO_KB, proj_c, vb0, rope_c, RO_KB,
                                    proj_c, vb0, no_sink_b, GA_HEADS, gb, 1.0, c_len, c_len, False)
            oc_c = _flash_attention(rope_c, RO_QC, rope_c, RO_KC, proj_c, vc0, rope_c, RO_KC,
                                    proj_c, vc0, sink_c[l] * LOG2E, WA_HEADS, gc, LOG2E, c_len,
                                    c_len, False)
            xc_mid = _merge(oc_a, oc_b, oc_c, proj_c, wa, wb, wc, wo, xc, mc[2], c_len)
            sel_c = _peer_select(xc_mid, g2, mc[3], mc[4], wq_t, keys, c_len)
            xc = _peer_experts(*sel_c, u=u, v_t=v_t, x=xc_mid, gate=mc[5], tm=c_len,
                               ec=PEER_EXPERT_CHUNK)

    return _final_norm(xs, final_g.reshape(1, d), tm_s).reshape(1, s_len, d)
```

```python
import functools

import jax
import jax.numpy as jnp
import numpy as np
from jax import lax
from jax.experimental import pallas as pl
from jax.experimental.pallas import tpu as pltpu

F32 = jnp.float32
BF16 = jnp.bfloat16

GRID_W = 64
HEAD_DIM = 128
NA_HEADS = 6
NA_KH = 8
NA_KW = 16
GA_HEADS = 6
GA_KV_HEADS = 2
WA_HEADS = 4
WA_KV_HEADS = 2
WINDOW = 128
ROPE_THETA = 10000.0
NORM_EPS = 1e-6
NEG = -1e30
LOG2E = 1.4426950408889634
PEER_HEADS = 8
PEER_NKEYS = 128
PEER_TOPK = 16
N_MOD = 6

A_W = NA_HEADS * HEAD_DIM
B_QW = GA_HEADS * HEAD_DIM
B_KW = GA_KV_HEADS * HEAD_DIM
C_QW = WA_HEADS * HEAD_DIM
C_KW = WA_KV_HEADS * HEAD_DIM

V7X_VMEM_BYTES = 64 * 1024 * 1024
VMEM_LIMIT = V7X_VMEM_BYTES - 8 * 1024 * 1024
LANE = 128
BF16_ROWS = 16

LOCAL_BLOCK = 256
PEER_EXPERT_CHUNK = 512


def _cparams(sem, flags=None):
    return pltpu.CompilerParams(dimension_semantics=sem, vmem_limit_bytes=VMEM_LIMIT, flags=flags)


def _dot(a, b):
    return jnp.dot(a, b, preferred_element_type=F32)


def _dot_nt(a, b):
    return lax.dot_general(a, b, (((1,), (1,)), ((), ())), preferred_element_type=F32)


def _norm_mod(x, g, sh, sc):
    ms = jnp.mean(x * x, axis=-1, keepdims=True)
    y = x * lax.rsqrt(ms + NORM_EPS) * g
    return y * (1.0 + sc) + sh


def _mod_kernel(c_ref, w_ref, b_ref, o_ref):
    c = c_ref[...]
    a = c / (1.0 + jnp.exp(-c))
    o_ref[...] = jnp.dot(a, w_ref[...], preferred_element_type=F32,
                         precision=lax.Precision.HIGHEST) + b_ref[...]


def _mod_vectors(c8, w_mod, b_mod):
    d, n = w_mod.shape
    tn = 1024
    return pl.pallas_call(
        _mod_kernel,
        out_shape=jax.ShapeDtypeStruct((8, n), F32),
        grid=(n // tn,),
        in_specs=[pl.BlockSpec((8, d), lambda j: (0, 0)),
                  pl.BlockSpec((d, tn), lambda j: (0, j)),
                  pl.BlockSpec((1, tn), lambda j: (0, j))],
        out_specs=pl.BlockSpec((8, tn), lambda j: (0, j)),
        compiler_params=_cparams(("arbitrary",)),
        name="mod_vectors",
    )(c8, w_mod, b_mod.reshape(1, n))


def _proj_kernel(x_ref, g_ref, sh_ref, sc_ref, w_ref, o_ref, h_scr):
    @pl.when(pl.program_id(1) == 0)
    def _():
        h_scr[...] = _norm_mod(x_ref[...], g_ref[...], sh_ref[...], sc_ref[...]).astype(BF16)

    o_ref[...] = _dot(h_scr[...], w_ref[...]).astype(o_ref.dtype)


def _norm_proj(x, g, sh, sc, w, tm, tn):
    r, d = x.shape
    n = w.shape[1]
    vec = pl.BlockSpec((1, d), lambda i, j: (0, 0))
    return pl.pallas_call(
        _proj_kernel,
        out_shape=jax.ShapeDtypeStruct((r, n), BF16),
        grid=(r // tm, n // tn),
        in_specs=[pl.BlockSpec((tm, d), lambda i, j: (i, 0)), vec, vec, vec,
                  pl.BlockSpec((d, tn), lambda i, j: (0, j))],
        out_specs=pl.BlockSpec((tm, tn), lambda i, j: (i, j)),
        scratch_shapes=[pltpu.VMEM((tm, d), BF16)],
        compiler_params=_cparams(("parallel", "arbitrary")),
        name="norm_proj",
    )(x, g, sh, sc, w)


def _rope_kernel(p0_ref, p1_ref, p2_ref, cos_ref, sin_ref, qn_ref, kn_ref, o_ref):
    cos = cos_ref[...]
    sin = sin_ref[...]
    lane = lax.broadcasted_iota(jnp.int32, cos.shape, 1)
    low = (lane % (HEAD_DIM // 2)) < (HEAD_DIM // 4)
    scale = HEAD_DIM ** -0.5
    src = jnp.concatenate([p0_ref[...], p1_ref[...], p2_ref[...]], axis=1).astype(F32)

    def head(col):
        return src[:, col * HEAD_DIM:(col + 1) * HEAD_DIM]

    def rms(x, g):
        ms = jnp.mean(x * x, axis=-1, keepdims=True)
        return x * lax.rsqrt(ms + NORM_EPS) * g

    def rope(x):
        partner = jnp.where(low, pltpu.roll(x, HEAD_DIM - HEAD_DIM // 4, 1),
                            pltpu.roll(x, HEAD_DIM // 4, 1))
        return x * cos + partner * sin

    outs = []
    for hq in range(GA_HEADS):
        outs.append(rope(rms(head(hq), qn_ref[...])) * (scale * LOG2E))
    for hk in range(GA_KV_HEADS):
        outs.append(rope(rms(head(GA_HEADS + hk), kn_ref[...])))
    c0 = GA_HEADS + 2 * GA_KV_HEADS
    for hq in range(WA_HEADS):
        outs.append(rope(head(c0 + hq)) * scale)
    for hk in range(WA_KV_HEADS):
        outs.append(rope(head(c0 + WA_HEADS + hk)))
    o_ref[...] = jnp.concatenate(outs, axis=1).astype(o_ref.dtype)


ROPE_OUT_HEADS = GA_HEADS + GA_KV_HEADS + WA_HEADS + WA_KV_HEADS
RO_QB, RO_KB, RO_QC, RO_KC = 0, GA_HEADS, GA_HEADS + GA_KV_HEADS, GA_HEADS + GA_KV_HEADS + WA_HEADS


def _rope_heads(proj, cos, sin, qn, kn, tm, bc_block0):
    r = proj.shape[0]
    wblk = 6 * HEAD_DIM
    pin = [pl.BlockSpec((tm, wblk), functools.partial(lambda i, b: (i, b), b=bc_block0 + t))
           for t in range(3)]
    tab = pl.BlockSpec((tm, HEAD_DIM), lambda i: (i, 0))
    vec = pl.BlockSpec((1, HEAD_DIM), lambda i: (0, 0))
    return pl.pallas_call(
        _rope_kernel,
        out_shape=jax.ShapeDtypeStruct((r, ROPE_OUT_HEADS * HEAD_DIM), BF16),
        grid=(r // tm,),
        in_specs=pin + [tab, tab, vec, vec],
        out_specs=pl.BlockSpec((tm, ROPE_OUT_HEADS * HEAD_DIM), lambda i: (i, 0)),
        compiler_params=_cparams(("parallel",)),
        name="qk_norm_rope",
    )(proj, proj, proj, cos, sin, qn, kn)


def _local_kernel(sink_ref, q_ref, k0_ref, k1_ref, k2_ref, v0_ref, v1_ref, v2_ref,
                  kc_ref, vc_ref, bias_ref, o_ref, *, scale):
    h = pl.program_id(0)
    q = q_ref[...]
    s = jnp.concatenate([_dot_nt(q, k0_ref[...]), _dot_nt(q, k1_ref[...]),
                         _dot_nt(q, k2_ref[...])], axis=1)
    sc = _dot_nt(q, kc_ref[...])
    if scale != 1.0:
        s = s * scale
        sc = sc * scale
    s = s + bias_ref[0, 0]
    m = jnp.maximum(jnp.max(s, axis=1, keepdims=True), jnp.max(sc, axis=1, keepdims=True))
    p = jnp.exp(s - m)
    pc = jnp.exp(sc - m)
    l = (jnp.sum(p, axis=1, keepdims=True) + jnp.sum(pc, axis=1, keepdims=True)
         + jnp.exp(sink_ref[h] - m))
    b = LOCAL_BLOCK
    pb = p.astype(BF16)
    o = (_dot(pb[:, 0:b], v0_ref[...]) + _dot(pb[:, b:2 * b], v1_ref[...])
         + _dot(pb[:, 2 * b:3 * b], v2_ref[...]) + _dot(pc.astype(BF16), vc_ref[...]))
    o_ref[...] = (o / l).astype(o_ref.dtype)


def _local_attention(q_arr, q_col0, k_arr, k_col0, v_arr, v_col0, kc_arr, kc_col0, vc_arr,
                     vc_col0, bias, sink, n_heads, group, scale):
    s_len = q_arr.shape[0]
    c_len = kc_arr.shape[0]
    b = LOCAL_BLOCK
    nb = s_len // b
    hb = bias.shape[1]

    def kv_spec(col0, off):
        return pl.BlockSpec(
            (b, HEAD_DIM), lambda h, i: (jnp.clip(i + off, 0, nb - 1), col0 + h // group))

    def variant(i):
        return jnp.where(i == 0, 0, jnp.where(i == nb - 1, 2, 1))

    in_specs = [
        pl.BlockSpec(memory_space=pltpu.SMEM),
        pl.BlockSpec((b, HEAD_DIM), lambda h, i: (i, q_col0 + h)),
        kv_spec(k_col0, -1), kv_spec(k_col0, 0), kv_spec(k_col0, 1),
        kv_spec(v_col0, -1), kv_spec(v_col0, 0), kv_spec(v_col0, 1),
        pl.BlockSpec((c_len, HEAD_DIM), lambda h, i: (0, kc_col0 + h // group)),
        pl.BlockSpec((c_len, HEAD_DIM), lambda h, i: (0, vc_col0 + h // group)),
        pl.BlockSpec((1, 1, b, 3 * b),
                     lambda h, i: (variant(i), h if hb > 1 else 0, 0, 0)),
    ]
    return pl.pallas_call(
        functools.partial(_local_kernel, scale=scale),
        out_shape=jax.ShapeDtypeStruct((s_len, n_heads * HEAD_DIM), BF16),
        grid=(n_heads, nb),
        in_specs=in_specs,
        out_specs=pl.BlockSpec((b, HEAD_DIM), lambda h, i: (i, h)),
        compiler_params=_cparams(("parallel", "arbitrary")),
        name="local_attention",
    )(sink, q_arr, k_arr, k_arr, k_arr, v_arr, v_arr, v_arr, kc_arr, vc_arr, bias)


def _flash_kernel(sink_ref, q_ref, k_ref, v_ref, kc_ref, vc_ref, o_ref, m_scr, acc_scr,
                  *, n_main, n_steps, scale, group):
    kvh = pl.program_id(0)
    j = pl.program_id(2)
    d = HEAD_DIM

    @pl.when(j == 0)
    def _():
        m_scr[...] = jnp.full(m_scr.shape, NEG, F32)
        acc_scr[...] = jnp.zeros(acc_scr.shape, F32)

    def step(k, v):
        tk = k.shape[0]
        v1 = jnp.concatenate([v, jnp.ones((tk, d), BF16)], axis=1)
        scores = [_dot_nt(q_ref[:, g * d:(g + 1) * d], k) for g in range(group)]
        for g in range(group):
            s = scores[g]
            if scale != 1.0:
                s = s * scale
            mx = s[:, 0:d]
            for t in range(1, tk // d):
                mx = jnp.maximum(mx, s[:, t * d:(t + 1) * d])
            m_prev = m_scr[g]
            m_new = jnp.maximum(m_prev, jnp.max(mx, axis=1, keepdims=True))
            alpha = jnp.exp2(m_prev - m_new)
            p = jnp.exp2(s - jnp.concatenate([m_new] * (tk // d), axis=1))
            acc_scr[g] = (jnp.concatenate([alpha, alpha], axis=1) * acc_scr[g]
                          + _dot(p.astype(BF16), v1))
            m_scr[g] = m_new

    @pl.when(j < n_main)
    def _():
        step(k_ref[...], v_ref[...])

    if n_steps > n_main:
        @pl.when(j == n_main)
        def _():
            step(kc_ref[...], vc_ref[...])

    @pl.when(j == n_steps - 1)
    def _():
        for g in range(group):
            acc = acc_scr[g]
            l = acc[:, d:] + jnp.exp2(sink_ref[kvh * group + g] - m_scr[g])
            o_ref[:, g * d:(g + 1) * d] = (acc[:, :d] / l).astype(o_ref.dtype)


def _flash_attention(q_arr, q_col0, k_arr, k_col0, v_arr, v_col0, kc_arr, kc_col0, vc_arr,
                     vc_col0, sink_log2, n_heads, group, scale, tq, tk, with_ctx):
    s_len = q_arr.shape[0]
    kv_len = k_arr.shape[0]
    c_len = kc_arr.shape[0]
    assert q_col0 % group == 0
    n_main = kv_len // tk
    n_steps = n_main + (1 if with_ctx else 0)
    last = n_main - 1
    qw = group * HEAD_DIM
    in_specs = [
        pl.BlockSpec(memory_space=pltpu.SMEM),
        pl.BlockSpec((tq, qw), lambda h, i, j: (i, q_col0 // group + h)),
        pl.BlockSpec((tk, HEAD_DIM), lambda h, i, j: (jnp.minimum(j, last), k_col0 + h)),
        pl.BlockSpec((tk, HEAD_DIM), lambda h, i, j: (jnp.minimum(j, last), v_col0 + h)),
        pl.BlockSpec((c_len, HEAD_DIM), lambda h, i, j: (0, kc_col0 + h)),
        pl.BlockSpec((c_len, HEAD_DIM), lambda h, i, j: (0, vc_col0 + h)),
    ]
    return pl.pallas_call(
        functools.partial(_flash_kernel, n_main=n_main, n_steps=n_steps, scale=scale, group=group),
        out_shape=jax.ShapeDtypeStruct((s_len, n_heads * HEAD_DIM), BF16),
        grid=(n_heads // group, s_len // tq, n_steps),
        in_specs=in_specs,
        out_specs=pl.BlockSpec((tq, qw), lambda h, i, j: (i, h)),
        scratch_shapes=[pltpu.VMEM((group, tq, HEAD_DIM), F32),
                        pltpu.VMEM((group, tq, 2 * HEAD_DIM), F32)],
        compiler_params=_cparams(("parallel", "parallel", "arbitrary")),
        name="flash_attention",
    )(sink_log2, q_arr, k_arr, v_arr, kc_arr, vc_arr)


def _merge_kernel(oa_ref, ob_ref, oc_ref, g0_ref, g1_ref, g2_ref, wa_ref, wb_ref, wc_ref,
                  wo_ref, x_ref, gate_ref, o_ref):
    def sig(g_ref):
        g = g_ref[...].astype(F32)
        return 1.0 / (1.0 + jnp.exp(-g))

    m = (sig(g0_ref) * _dot(oa_ref[...], wa_ref[...])
         + sig(g1_ref) * _dot(ob_ref[...], wb_ref[...])
         + sig(g2_ref) * _dot(oc_ref[...], wc_ref[...]))
    y = _dot(m.astype(BF16), wo_ref[...])
    o_ref[...] = x_ref[...] + gate_ref[...] * y


def _merge(oa, ob, oc, proj, wa, wb, wc, wo, x, gate, tm):
    r, d = x.shape

    def rows(w):
        return pl.BlockSpec((tm, w), lambda i: (i, 0))

    def full(a):
        return pl.BlockSpec(a.shape, lambda i: (0, 0))

    gates = [pl.BlockSpec((tm, d), functools.partial(lambda i, b: (i, b), b=t)) for t in range(3)]
    return pl.pallas_call(
        _merge_kernel,
        out_shape=jax.ShapeDtypeStruct((r, d), F32),
        grid=(r // tm,),
        in_specs=[rows(oa.shape[1]), rows(ob.shape[1]), rows(oc.shape[1])] + gates
        + [full(wa), full(wb), full(wc), full(wo), rows(d), pl.BlockSpec((1, d), lambda i: (0, 0))],
        out_specs=rows(d),
        compiler_params=_cparams(("parallel",)),
        name="merge_branches",
    )(oa, ob, oc, proj, proj, proj, wa, wb, wc, wo, x, gate)


def _topk_ranks(s):
    n_rows = s.shape[0]
    prio = (n_rows - lax.broadcasted_iota(jnp.int32, s.shape, 0)).astype(F32)
    rank = jnp.full(s.shape, float(PEER_TOPK), F32)
    vals = []
    for r in range(PEER_TOPK):
        m = jnp.max(s, axis=0, keepdims=True)
        tied = jnp.where(s == m, prio, 0.0)
        sel = tied == jnp.max(tied, axis=0, keepdims=True)
        rank = jnp.where(sel, float(r), rank)
        s = jnp.where(sel, -jnp.inf, s)
        vals.append(m)
    return rank, vals


def _peer_select_kernel(x_ref, g_ref, sh_ref, sc_ref, wq_ref, keys_ref,
                        ht_ref, n1_ref, a_ref, rk2_ref, b_ref):
    h2 = _norm_mod(x_ref[...], g_ref[...], sh_ref[...], sc_ref[...])
    h2t = h2.T.astype(BF16)
    ht_ref[...] = h2t
    qt = _dot(wq_ref[...], h2t)
    k = PEER_TOPK
    for h in range(PEER_HEADS):
        sv = []
        for p in range(2):
            hp = 2 * h + p
            sv.append(jnp.dot(keys_ref[hp], qt[hp * PEER_NKEYS:(hp + 1) * PEER_NKEYS, :],
                              preferred_element_type=F32, precision=lax.Precision.HIGHEST))
        rank1, d1 = _topk_ranks(sv[0])
        rank2, d2 = _topk_ranks(sv[1])
        hk = k // 2
        d2_all = jnp.concatenate(d2, axis=0)
        d2_low = d2_all[:hk]
        cand = jnp.concatenate([d1[0] + d2_all] + [d1[a] + d2_low for a in range(1, hk)]
                               + [jnp.concatenate(d1[hk:], axis=0) + d2[0]], axis=0)
        rank_c, _ = _topk_ranks(cand)
        chosen = rank_c < float(k)
        c_max = d1[0] + d2[0]
        z = jnp.sum(jnp.where(chosen, jnp.exp(cand - c_max), 0.0), axis=0, keepdims=True)
        picked = jnp.where(chosen, 1.0, 0.0)
        counts = [jnp.sum(picked[0:k], axis=0, keepdims=True)]
        counts += [jnp.sum(picked[k + hk * (a - 1):k + hk * a], axis=0, keepdims=True)
                   for a in range(1, hk)]
        counts += [picked[k + hk * (hk - 1) + a:k + hk * (hk - 1) + a + 1] for a in range(hk)]
        n1 = jnp.zeros(rank1.shape, F32)
        for a in range(k):
            n1 = n1 + jnp.where(rank1 == float(a), counts[a], 0.0)
        n1_ref[h] = n1
        a_ref[h] = jnp.exp(sv[0] - d1[0]) / z
        rk2_ref[h] = rank2.astype(BF16)
        b_ref[h] = jnp.exp(sv[1] - d2[0]).astype(BF16)


def _peer_select(x, g, sh, sc, wq_t, keys, tm):
    r, d = x.shape
    vec = pl.BlockSpec((1, d), lambda i: (0, 0))
    tab = jax.ShapeDtypeStruct((PEER_HEADS, PEER_NKEYS, r), F32)
    tab16 = jax.ShapeDtypeStruct((PEER_HEADS, PEER_NKEYS, r), BF16)
    tab_spec = pl.BlockSpec((PEER_HEADS, PEER_NKEYS, tm), lambda i: (0, 0, i))
    return pl.pallas_call(
        _peer_select_kernel,
        out_shape=(jax.ShapeDtypeStruct((d, r), BF16), tab, tab, tab16, tab16),
        grid=(r // tm,),
        in_specs=[pl.BlockSpec((tm, d), lambda i: (i, 0)), vec, vec, vec,
                  pl.BlockSpec(wq_t.shape, lambda i: (0, 0)),
                  pl.BlockSpec(keys.shape, lambda i: (0, 0, 0))],
        out_specs=(pl.BlockSpec((d, tm), lambda i: (0, i)), tab_spec, tab_spec, tab_spec, tab_spec),
        compiler_params=_cparams(("parallel",)),
        name="peer_select",
    )(x, g, sh, sc, wq_t, keys)


def _peer_expert_kernel(ht_ref, u_ref, vt_ref, n1_ref, a_ref, rk2_ref, b_ref, x_ref, gate_ref,
                        o_ref, acc_scr, act_scr, c_scr, *, n_chunk):
    kk = pl.program_id(1)
    tm = ht_ref.shape[1]
    nk = PEER_NKEYS

    @pl.when(kk == 0)
    def _():
        acc_scr[...] = jnp.zeros(acc_scr.shape, F32)
        act_scr[...] = jnp.zeros(act_scr.shape, F32)
        c_scr[...] = jnp.zeros(c_scr.shape, BF16)

    def step(cur, prev):
        act_scr[cur] = _dot(u_ref[...], ht_ref[...])
        dm = acc_scr.shape[0] // n_chunk
        for c in range(n_chunk):
            rows = slice(c * nk, (c + 1) * nk)
            for l0 in range(0, tm, LANE):
                lanes = slice(l0, l0 + LANE)
                coef = None
                for h in range(PEER_HEADS):
                    row = prev * n_chunk + c
                    n = jnp.broadcast_to(n1_ref[h, row:row + 1, lanes], (nk, LANE)).astype(BF16)
                    a = jnp.broadcast_to(a_ref[h, row:row + 1, lanes], (nk, LANE)).astype(BF16)
                    gate = jnp.minimum(jnp.maximum(n - rk2_ref[h, :, lanes], 0.0), a)
                    term = gate * b_ref[h, :, lanes]
                    coef = term if coef is None else coef + term
                act = act_scr[prev, rows, lanes]
                c_scr[prev, rows, lanes] = jax.nn.gelu(act).astype(BF16) * coef
            out_rows = slice(c * dm, (c + 1) * dm)
            acc_scr[out_rows, :] += _dot(vt_ref[out_rows, :], c_scr[cur])

    @pl.when(kk % 2 == 0)
    def _():
        step(0, 1)

    @pl.when(kk % 2 == 1)
    def _():
        step(1, 0)

    @pl.when(kk == pl.num_programs(1) - 1)
    def _():
        o_ref[...] = x_ref[...] + gate_ref[...] * acc_scr[...].T


PEER_PIPE = 2


def _peer_experts(h_t, n1, a, rk2, b, u, v_t, x, gate, tm, ec):
    d, r = h_t.shape
    e = u.shape[0]
    n_chunk = ec // PEER_NKEYS
    nke = e // ec
    last = nke - 1
    assert 2 * n_chunk == 8
    tab_spec = pl.BlockSpec((PEER_HEADS, PEER_NKEYS, tm), lambda i, k: (0, 0, i))
    row_spec = pl.BlockSpec((PEER_HEADS, 2 * n_chunk, tm),
                            lambda i, k: (0, jnp.clip(k - 1, 0, last) // 2, i))
    return pl.pallas_call(
        functools.partial(_peer_expert_kernel, n_chunk=n_chunk),
        out_shape=jax.ShapeDtypeStruct((r, d), F32),
        grid=(r // tm, nke + PEER_PIPE),
        in_specs=[pl.BlockSpec((d, tm), lambda i, k: (0, i)),
                  pl.BlockSpec((ec, d), lambda i, k: (jnp.minimum(k, last), 0)),
                  pl.BlockSpec((d, ec), lambda i, k: (0, jnp.clip(k - 2, 0, last))),
                  row_spec, row_spec, tab_spec, tab_spec,
                  pl.BlockSpec((tm, d), lambda i, k: (i, 0)),
                  pl.BlockSpec((1, d), lambda i, k: (0, 0))],
        out_specs=pl.BlockSpec((tm, d), lambda i, k: (i, 0)),
        scratch_shapes=[pltpu.VMEM((d, tm), F32), pltpu.VMEM((2, ec, tm), F32),
                        pltpu.VMEM((2, ec, tm), BF16)],
        compiler_params=_cparams(("parallel", "arbitrary")),
        name="peer_experts",
    )(h_t, u, v_t, n1, a, rk2, b, x, gate)


def _final_kernel(x_ref, g_ref, o_ref):
    x = x_ref[...]
    ms = jnp.mean(x * x, axis=-1, keepdims=True)
    o_ref[...] = x * lax.rsqrt(ms + NORM_EPS) * g_ref[...]


def _final_norm(x, g, tm):
    r, d = x.shape
    return pl.pallas_call(
        _final_kernel,
        out_shape=jax.ShapeDtypeStruct((r, d), F32),
        grid=(r // tm,),
        in_specs=[pl.BlockSpec((tm, d), lambda i: (i, 0)), pl.BlockSpec((1, d), lambda i: (0, 0))],
        out_specs=pl.BlockSpec((tm, d), lambda i: (i, 0)),
        compiler_params=_cparams(("parallel",)),
        name="final_norm",
    )(x, g)


def _rope_tables(s_len):
    quarter = HEAD_DIM // 4
    t = jnp.arange(s_len)
    pos_row = (t // GRID_W).astype(F32)
    pos_col = (t % GRID_W).astype(F32)
    inv_freq = ROPE_THETA ** (-jnp.arange(quarter, dtype=F32) / quarter)
    ar = pos_row[:, None] * inv_freq[None, :]
    ac = pos_col[:, None] * inv_freq[None, :]
    cos = jnp.concatenate([jnp.cos(ar), jnp.cos(ar), jnp.cos(ac), jnp.cos(ac)], axis=1)
    sin = jnp.concatenate([-jnp.sin(ar), jnp.sin(ar), -jnp.sin(ac), jnp.sin(ac)], axis=1)
    return cos, sin


def _local_geometry(rows):
    b = LOCAL_BLOCK
    rows_per_block = b // GRID_W
    kh = min(NA_KH, rows)
    q = np.arange(b)
    k = np.arange(3 * b)
    q_row_l, q_col = q // GRID_W, q % GRID_W
    k_row_l, k_col = k // GRID_W - rows_per_block, k % GRID_W
    nb = rows // rows_per_block
    geo = []
    for block in (0, 1, nb - 1):
        q_row = block * rows_per_block + q_row_l
        k_row = block * rows_per_block + k_row_l
        geo.append((q_row, q_col, k_row, k_col))
    return kh, geo


def _na_bias(rpb, rows):
    kh, geo = _local_geometry(rows)
    rpb_rows = LOCAL_BLOCK // GRID_W
    qr, kr = np.arange(rpb_rows), np.arange(3 * rpb_rows) - rpb_rows
    row_sel = (kr[None, :, None] - qr[:, None, None] + (NA_KH - 1)
               == np.arange(2 * NA_KH - 1)[None, None, :]).astype(np.float32)
    col = np.arange(GRID_W)
    col_sel = (col[None, :, None] - col[:, None, None] + (NA_KW - 1)
               == np.arange(2 * NA_KW - 1)[None, None, :]).astype(np.float32)
    hi = lax.Precision.HIGHEST
    t1 = jnp.einsum('hab,cdb->hacd', rpb.astype(F32), col_sel, precision=hi)
    bias = jnp.einsum('hacd,qka->hqckd', t1, row_sel, precision=hi)
    bias = bias.reshape(rpb.shape[0], LOCAL_BLOCK, 3 * LOCAL_BLOCK)
    out = []
    for q_row, q_col, k_row, k_col in geo:
        rs = np.clip(q_row - kh // 2, 0, rows - kh)
        cs = np.clip(q_col - NA_KW // 2, 0, GRID_W - NA_KW)
        ok = ((k_row[None, :] >= rs[:, None]) & (k_row[None, :] < rs[:, None] + kh)
              & (k_col[None, :] >= cs[:, None]) & (k_col[None, :] < cs[:, None] + NA_KW)
              & (k_row[None, :] >= 0) & (k_row[None, :] < rows))
        out.append(jnp.where(jnp.asarray(ok)[None], bias, NEG))
    return jnp.stack(out, axis=0)


def _window_mask(rows):
    _, geo = _local_geometry(rows)
    out = []
    for q_row, q_col, k_row, k_col in geo:
        qp = q_row * GRID_W + q_col
        kp = k_row * GRID_W + k_col
        ok = (np.abs(kp[None, :] - qp[:, None]) <= WINDOW) & (kp[None, :] >= 0) \
            & (kp[None, :] < rows * GRID_W)
        out.append(np.where(ok, 0.0, NEG)[None])
    return jnp.asarray(np.stack(out, axis=0), F32)


def _row_tile(r, pref):
    return pref if r % pref == 0 else LOCAL_BLOCK


def kernel(x, c, ctx, c_ctx, w_mod, b_mod, norm1_g, norm2_g, w_in, rpb_a, qnorm_b, knorm_b, sink_c,
           w_br_a, w_br_b, w_br_c, w_out, peer_wq, peer_keys, peer_u, peer_v, final_g):
    _, s_len, d = x.shape
    c_len = ctx.shape[1]
    depth = w_mod.shape[0]
    rows = s_len // GRID_W
    xs = x[0]
    xc = ctx[0]

    cos, sin = _rope_tables(s_len)
    cos_c = jnp.ones((c_len, HEAD_DIM), F32)
    sin_c = jnp.zeros((c_len, HEAD_DIM), F32)
    win_mask = _window_mask(rows)
    no_sink_a = jnp.full((NA_HEADS,), NEG, F32)
    no_sink_b = jnp.full((GA_HEADS,), NEG, F32)
    c8 = jnp.zeros((8, d), F32).at[0].set(c_ctx).at[1].set(c[0])

    g_blk = 3 * d // HEAD_DIM
    qa0, ka0, va0 = g_blk, g_blk + NA_HEADS, g_blk + 2 * NA_HEADS
    bc0 = g_blk + 3 * NA_HEADS
    vb0 = bc0 + GA_HEADS + GA_KV_HEADS
    vc0 = vb0 + GA_KV_HEADS + WA_HEADS + WA_KV_HEADS
    n_qkv = 3 * A_W + B_QW + 2 * B_KW + C_QW + 2 * C_KW

    tm_s = _row_tile(s_len, 512)
    gb = GA_HEADS // GA_KV_HEADS
    gc = WA_HEADS // WA_KV_HEADS
    for l in range(depth):
        ctx_out = l < depth - 1
        w_in_l = jnp.concatenate([w_in[l][:, n_qkv:], w_in[l][:, :n_qkv]], axis=1).astype(BF16)
        mod = _mod_vectors(c8, w_mod[l], b_mod[l])
        mc = [mod[0:1, t * d:(t + 1) * d] for t in range(N_MOD)]
        ms = [mod[1:2, t * d:(t + 1) * d] for t in range(N_MOD)]
        g1 = norm1_g[l].reshape(1, d)
        g2 = norm2_g[l].reshape(1, d)

        proj = _norm_proj(xs, g1, ms[0], ms[1], w_in_l, tm_s, 1536)
        proj_c = _norm_proj(xc, g1, mc[0], mc[1], w_in_l, c_len, 1536)
        qn = qnorm_b[l].reshape(1, HEAD_DIM)
        kn = knorm_b[l].reshape(1, HEAD_DIM)
        bc_block0 = bc0 // 6
        rope = _rope_heads(proj, cos, sin, qn, kn, tm_s, bc_block0)
        rope_c = _rope_heads(proj_c, cos_c, sin_c, qn, kn, c_len, bc_block0)

        na_scale = HEAD_DIM ** -0.5
        o_a = _local_attention(proj, qa0, proj, ka0, proj, va0, proj_c, ka0, proj_c, va0,
                               _na_bias(rpb_a[l], rows), no_sink_a, NA_HEADS, 1, na_scale)
        o_b = _flash_attention(rope, RO_QB, rope, RO_KB, proj, vb0, rope_c, RO_KB, proj_c, vb0,
                               no_sink_b, GA_HEADS, gb, 1.0, tm_s, _row_tile(s_len, 1024), True)
        o_c = _local_attention(rope, RO_QC, rope, RO_KC, proj, vc0, rope_c, RO_KC, proj_c, vc0,
                               win_mask, sink_c[l], WA_HEADS, gc, 1.0)
        wa = w_br_a[l].astype(BF16)
        wb = w_br_b[l].astype(BF16)
        wc = w_br_c[l].astype(BF16)
        wo = w_out[l].astype(BF16)
        x_mid = _merge(o_a, o_b, o_c, proj, wa, wb, wc, wo, xs, ms[2], LOCAL_BLOCK)

        wq_t = peer_wq[l].T.astype(BF16)
        keys = peer_keys[l].reshape(2 * PEER_HEADS, PEER_NKEYS, -1)
        u = peer_u[l].astype(BF16)
        v_t = peer_v[l].T.astype(BF16)
        sel = _peer_select(x_mid, g2, ms[3], ms[4], wq_t, keys, LOCAL_BLOCK)
        xs = _peer_experts(*sel, u=u, v_t=v_t, x=x_mid, gate=ms[5], tm=tm_s,
                           ec=PEER_EXPERT_CHUNK)

        if ctx_out:
            oc_a = _flash_attention(proj_c, qa0, proj_c, ka0, proj_c, va0, proj_c, ka0, proj_c,
                                    va0, no_sink_a, NA_HEADS, 1, na_scale * LOG2E, c_len, c_len,
                                    False)
            oc_b = _flash_attention(rope_c, RO_QB, rope_c, RO_KB, proj_c, vb0, rope_c, RO_KB,
                                    proj_c, vb0, no_sink_b, GA_HEADS, gb, 1.0, c_len, c_len, False)
            oc_c = _flash_attention(rope_c, RO_QC, rope_c, RO_KC, proj_c, vc0, rope_c, RO_KC,
                                    proj_c, vc0, sink_c[l] * LOG2E, WA_HEADS, gc, LOG2E, c_len,
                                    c_len, False)
            xc_mid = _merge(oc_a, oc_b, oc_c, proj_c, wa, wb, wc, wo, xc, mc[2], c_len)
            sel_c = _peer_select(xc_mid, g2, mc[3], mc[4], wq_t, keys, c_len)
            xc = _peer_experts(*sel_c, u=u, v_t=v_t, x=xc_mid, gate=mc[5], tm=c_len,
                               ec=PEER_EXPERT_CHUNK)

    return _final_norm(xs, final_g.reshape(1, d), tm_s).reshape(1, s_len, d)
```

```python
import functools

import jax
import jax.numpy as jnp
import numpy as np
from jax import lax
from jax.experimental import pallas as pl
from jax.experimental.pallas import tpu as pltpu

F32 = jnp.float32
BF16 = jnp.bfloat16

GRID_W = 64
HEAD_DIM = 128
NA_HEADS = 6
NA_KH = 8
NA_KW = 16
GA_HEADS = 6
GA_KV_HEADS = 2
WA_HEADS = 4
WA_KV_HEADS = 2
WINDOW = 128
ROPE_THETA = 10000.0
NORM_EPS = 1e-6
NEG = -1e30
LOG2E = 1.4426950408889634
PEER_HEADS = 8
PEER_NKEYS = 128
PEER_TOPK = 16
N_MOD = 6

A_W = NA_HEADS * HEAD_DIM
B_QW = GA_HEADS * HEAD_DIM
B_KW = GA_KV_HEADS * HEAD_DIM
C_QW = WA_HEADS * HEAD_DIM
C_KW = WA_KV_HEADS * HEAD_DIM

V7X_VMEM_BYTES = 64 * 1024 * 1024
VMEM_LIMIT = V7X_VMEM_BYTES - 8 * 1024 * 1024
LANE = 128
BF16_ROWS = 16

LOCAL_BLOCK = 256
FLASH_LOOKAHEAD = 2
PEER_STAGE = 512
PEER_EXPERT_CHUNK = 1024


def _cparams(sem, flags=None):
    return pltpu.CompilerParams(dimension_semantics=sem, vmem_limit_bytes=VMEM_LIMIT, flags=flags)


def _dot(a, b):
    return jnp.dot(a, b, preferred_element_type=F32)


def _dot_nt(a, b):
    return lax.dot_general(a, b, (((1,), (1,)), ((), ())), preferred_element_type=F32)


def _norm_mod(x, g, sh, sc):
    ms = jnp.mean(x * x, axis=-1, keepdims=True)
    y = x * lax.rsqrt(ms + NORM_EPS) * g
    return y * (1.0 + sc) + sh


def _mod_kernel(c_ref, w_ref, b_ref, o_ref):
    c = c_ref[...]
    a = c / (1.0 + jnp.exp(-c))
    o_ref[...] = jnp.dot(a, w_ref[...], preferred_element_type=F32,
                         precision=lax.Precision.HIGHEST) + b_ref[...]


def _mod_vectors(c8, w_mod, b_mod):
    d, n = w_mod.shape
    tn = 1024
    return pl.pallas_call(
        _mod_kernel,
        out_shape=jax.ShapeDtypeStruct((8, n), F32),
        grid=(n // tn,),
        in_specs=[pl.BlockSpec((8, d), lambda j: (0, 0)),
                  pl.BlockSpec((d, tn), lambda j: (0, j)),
                  pl.BlockSpec((1, tn), lambda j: (0, j))],
        out_specs=pl.BlockSpec((8, tn), lambda j: (0, j)),
        compiler_params=_cparams(("arbitrary",)),
        name="mod_vectors",
    )(c8, w_mod, b_mod.reshape(1, n))


def _proj_kernel(x_ref, g_ref, sh_ref, sc_ref, w_ref, o_ref, h_scr):
    @pl.when(pl.program_id(1) == 0)
    def _():
        h_scr[...] = _norm_mod(x_ref[...], g_ref[...], sh_ref[...], sc_ref[...]).astype(BF16)

    o_ref[...] = _dot(h_scr[...], w_ref[...]).astype(o_ref.dtype)


def _norm_proj(x, g, sh, sc, w, tm, tn):
    r, d = x.shape
    n = w.shape[1]
    vec = pl.BlockSpec((1, d), lambda i, j: (0, 0))
    return pl.pallas_call(
        _proj_kernel,
        out_shape=jax.ShapeDtypeStruct((r, n), BF16),
        grid=(r // tm, n // tn),
        in_specs=[pl.BlockSpec((tm, d), lambda i, j: (i, 0)), vec, vec, vec,
                  pl.BlockSpec((d, tn), lambda i, j: (0, j))],
        out_specs=pl.BlockSpec((tm, tn), lambda i, j: (i, j)),
        scratch_shapes=[pltpu.VMEM((tm, d), BF16)],
        compiler_params=_cparams(("parallel", "arbitrary")),
        name="norm_proj",
    )(x, g, sh, sc, w)


def _rope_kernel(p0_ref, p1_ref, p2_ref, cos_ref, sin_ref, qn_ref, kn_ref, o_ref):
    cos = cos_ref[...]
    sin = sin_ref[...]
    lane = lax.broadcasted_iota(jnp.int32, cos.shape, 1)
    low = (lane % (HEAD_DIM // 2)) < (HEAD_DIM // 4)
    scale = HEAD_DIM ** -0.5
    src = jnp.concatenate([p0_ref[...], p1_ref[...], p2_ref[...]], axis=1).astype(F32)

    def head(col):
        return src[:, col * HEAD_DIM:(col + 1) * HEAD_DIM]

    def rms(x, g):
        ms = jnp.mean(x * x, axis=-1, keepdims=True)
        return x * lax.rsqrt(ms + NORM_EPS) * g

    def rope(x):
        partner = jnp.where(low, pltpu.roll(x, HEAD_DIM - HEAD_DIM // 4, 1),
                            pltpu.roll(x, HEAD_DIM // 4, 1))
        return x * cos + partner * sin

    outs = []
    for hq in range(GA_HEADS):
        outs.append(rope(rms(head(hq), qn_ref[...])) * (scale * LOG2E))
    for hk in range(GA_KV_HEADS):
        outs.append(rope(rms(head(GA_HEADS + hk), kn_ref[...])))
    c0 = GA_HEADS + 2 * GA_KV_HEADS
    for hq in range(WA_HEADS):
        outs.append(rope(head(c0 + hq)) * scale)
    for hk in range(WA_KV_HEADS):
        outs.append(rope(head(c0 + WA_HEADS + hk)))
    o_ref[...] = jnp.concatenate(outs, axis=1).astype(o_ref.dtype)


ROPE_OUT_HEADS = GA_HEADS + GA_KV_HEADS + WA_HEADS + WA_KV_HEADS
RO_QB, RO_KB, RO_QC, RO_KC = 0, GA_HEADS, GA_HEADS + GA_KV_HEADS, GA_HEADS + GA_KV_HEADS + WA_HEADS


def _rope_heads(proj, cos, sin, qn, kn, tm, bc_block0):
    r = proj.shape[0]
    wblk = 6 * HEAD_DIM
    pin = [pl.BlockSpec((tm, wblk), functools.partial(lambda i, b: (i, b), b=bc_block0 + t))
           for t in range(3)]
    tab = pl.BlockSpec((tm, HEAD_DIM), lambda i: (i, 0))
    vec = pl.BlockSpec((1, HEAD_DIM), lambda i: (0, 0))
    return pl.pallas_call(
        _rope_kernel,
        out_shape=jax.ShapeDtypeStruct((r, ROPE_OUT_HEADS * HEAD_DIM), BF16),
        grid=(r // tm,),
        in_specs=pin + [tab, tab, vec, vec],
        out_specs=pl.BlockSpec((tm, ROPE_OUT_HEADS * HEAD_DIM), lambda i: (i, 0)),
        compiler_params=_cparams(("parallel",)),
        name="qk_norm_rope",
    )(proj, proj, proj, cos, sin, qn, kn)


def _local_kernel(sink_ref, q_ref, k0_ref, k1_ref, k2_ref, v0_ref, v1_ref, v2_ref,
                  kc_ref, vc_ref, bias_ref, o_ref, *, scale):
    h = pl.program_id(0)
    q = q_ref[...]
    s = jnp.concatenate([_dot_nt(q, k0_ref[...]), _dot_nt(q, k1_ref[...]),
                         _dot_nt(q, k2_ref[...])], axis=1)
    sc = _dot_nt(q, kc_ref[...])
    if scale != 1.0:
        s = s * scale
        sc = sc * scale
    s = s + bias_ref[0, 0]
    m = jnp.maximum(jnp.max(s, axis=1, keepdims=True), jnp.max(sc, axis=1, keepdims=True))
    p = jnp.exp(s - m)
    pc = jnp.exp(sc - m)
    l = (jnp.sum(p, axis=1, keepdims=True) + jnp.sum(pc, axis=1, keepdims=True)
         + jnp.exp(sink_ref[h] - m))
    b = LOCAL_BLOCK
    pb = p.astype(BF16)
    o = (_dot(pb[:, 0:b], v0_ref[...]) + _dot(pb[:, b:2 * b], v1_ref[...])
         + _dot(pb[:, 2 * b:3 * b], v2_ref[...]) + _dot(pc.astype(BF16), vc_ref[...]))
    o_ref[...] = (o / l).astype(o_ref.dtype)


def _local_attention(q_arr, q_col0, k_arr, k_col0, v_arr, v_col0, kc_arr, kc_col0, vc_arr,
                     vc_col0, bias, sink, n_heads, group, scale):
    s_len = q_arr.shape[0]
    c_len = kc_arr.shape[0]
    b = LOCAL_BLOCK
    nb = s_len // b
    hb = bias.shape[1]

    def kv_spec(col0, off):
        return pl.BlockSpec(
            (b, HEAD_DIM), lambda h, i: (jnp.clip(i + off, 0, nb - 1), col0 + h // group))

    def variant(i):
        return jnp.where(i == 0, 0, jnp.where(i == nb - 1, 2, 1))

    in_specs = [
        pl.BlockSpec(memory_space=pltpu.SMEM),
        pl.BlockSpec((b, HEAD_DIM), lambda h, i: (i, q_col0 + h)),
        kv_spec(k_col0, -1), kv_spec(k_col0, 0), kv_spec(k_col0, 1),
        kv_spec(v_col0, -1), kv_spec(v_col0, 0), kv_spec(v_col0, 1),
        pl.BlockSpec((c_len, HEAD_DIM), lambda h, i: (0, kc_col0 + h // group)),
        pl.BlockSpec((c_len, HEAD_DIM), lambda h, i: (0, vc_col0 + h // group)),
        pl.BlockSpec((1, 1, b, 3 * b),
                     lambda h, i: (variant(i), h if hb > 1 else 0, 0, 0)),
    ]
    return pl.pallas_call(
        functools.partial(_local_kernel, scale=scale),
        out_shape=jax.ShapeDtypeStruct((s_len, n_heads * HEAD_DIM), BF16),
        grid=(n_heads, nb),
        in_specs=in_specs,
        out_specs=pl.BlockSpec((b, HEAD_DIM), lambda h, i: (i, h)),
        compiler_params=_cparams(("parallel", "arbitrary")),
        name="local_attention",
    )(sink, q_arr, k_arr, k_arr, k_arr, v_arr, v_arr, v_arr, kc_arr, vc_arr, bias)


def _flash_kernel(sink_ref, q_ref, k_ref, v_ref, kc_ref, vc_ref, o_ref, m_scr, acc_scr,
                  *, n_main, n_steps, scale, group):
    kvh = pl.program_id(0)
    j = pl.program_id(2)
    d = HEAD_DIM

    @pl.when(j == 0)
    def _():
        m_scr[...] = jnp.full(m_scr.shape, NEG, F32)
        acc_scr[...] = jnp.zeros(acc_scr.shape, F32)

    def step(k, v):
        tk = k.shape[0]
        v1 = jnp.concatenate([v, jnp.ones((tk, d), BF16)], axis=1)
        def score(g):
            return _dot_nt(q_ref[:, g * d:(g + 1) * d], k)

        scores = {g: score(g) for g in range(min(FLASH_LOOKAHEAD, group))}
        for g in range(group):
            s = scores.pop(g)
            if scale != 1.0:
                s = s * scale
            mx = s[:, 0:d]
            for t in range(1, tk // d):
                mx = jnp.maximum(mx, s[:, t * d:(t + 1) * d])
            m_prev = m_scr[g]
            m_new = jnp.maximum(m_prev, jnp.max(mx, axis=1, keepdims=True))
            alpha = jnp.exp2(m_prev - m_new)
            p = jnp.exp2(s - jnp.concatenate([m_new] * (tk // d), axis=1)).astype(BF16)
            m_scr[g] = m_new
            if g + FLASH_LOOKAHEAD < group:
                scores[g + FLASH_LOOKAHEAD] = score(g + FLASH_LOOKAHEAD)
            acc_scr[g] = jnp.concatenate([alpha, alpha], axis=1) * acc_scr[g] + _dot(p, v1)

    @pl.when(j < n_main)
    def _():
        step(k_ref[...], v_ref[...])

    if n_steps > n_main:
        @pl.when(j == n_main)
        def _():
            step(kc_ref[...], vc_ref[...])

    @pl.when(j == n_steps - 1)
    def _():
        for g in range(group):
            acc = acc_scr[g]
            l = acc[:, d:] + jnp.exp2(sink_ref[kvh * group + g] - m_scr[g])
            o_ref[:, g * d:(g + 1) * d] = (acc[:, :d] / l).astype(o_ref.dtype)


def _flash_attention(q_arr, q_col0, k_arr, k_col0, v_arr, v_col0, kc_arr, kc_col0, vc_arr,
                     vc_col0, sink_log2, n_heads, group, scale, tq, tk, with_ctx):
    s_len = q_arr.shape[0]
    kv_len = k_arr.shape[0]
    c_len = kc_arr.shape[0]
    assert q_col0 % group == 0
    n_main = kv_len // tk
    n_steps = n_main + (1 if with_ctx else 0)
    last = n_main - 1
    qw = group * HEAD_DIM
    in_specs = [
        pl.BlockSpec(memory_space=pltpu.SMEM),
        pl.BlockSpec((tq, qw), lambda h, i, j: (i, q_col0 // group + h)),
        pl.BlockSpec((tk, HEAD_DIM), lambda h, i, j: (jnp.minimum(j, last), k_col0 + h)),
        pl.BlockSpec((tk, HEAD_DIM), lambda h, i, j: (jnp.minimum(j, last), v_col0 + h)),
        pl.BlockSpec((c_len, HEAD_DIM), lambda h, i, j: (0, kc_col0 + h)),
        pl.BlockSpec((c_len, HEAD_DIM), lambda h, i, j: (0, vc_col0 + h)),
    ]
    return pl.pallas_call(
        functools.partial(_flash_kernel, n_main=n_main, n_steps=n_steps, scale=scale, group=group),
        out_shape=jax.ShapeDtypeStruct((s_len, n_heads * HEAD_DIM), BF16),
        grid=(n_heads // group, s_len // tq, n_steps),
        in_specs=in_specs,
        out_specs=pl.BlockSpec((tq, qw), lambda h, i, j: (i, h)),
        scratch_shapes=[pltpu.VMEM((group, tq, HEAD_DIM), F32),
                        pltpu.VMEM((group, tq, 2 * HEAD_DIM), F32)],
        compiler_params=_cparams(("parallel", "parallel", "arbitrary")),
        name="flash_attention",
    )(sink_log2, q_arr, k_arr, v_arr, kc_arr, vc_arr)


def _merge_kernel(oa_ref, ob_ref, oc_ref, g0_ref, g1_ref, g2_ref, wa_ref, wb_ref, wc_ref,
                  wo_ref, x_ref, gate_ref, o_ref):
    def sig(g_ref):
        g = g_ref[...].astype(F32)
        return 1.0 / (1.0 + jnp.exp(-g))

    m = (sig(g0_ref) * _dot(oa_ref[...], wa_ref[...])
         + sig(g1_ref) * _dot(ob_ref[...], wb_ref[...])
         + sig(g2_ref) * _dot(oc_ref[...], wc_ref[...]))
    y = _dot(m.astype(BF16), wo_ref[...])
    o_ref[...] = x_ref[...] + gate_ref[...] * y


def _merge(oa, ob, oc, proj, wa, wb, wc, wo, x, gate, tm):
    r, d = x.shape

    def rows(w):
        return pl.BlockSpec((tm, w), lambda i: (i, 0))

    def full(a):
        return pl.BlockSpec(a.shape, lambda i: (0, 0))

    gates = [pl.BlockSpec((tm, d), functools.partial(lambda i, b: (i, b), b=t)) for t in range(3)]
    return pl.pallas_call(
        _merge_kernel,
        out_shape=jax.ShapeDtypeStruct((r, d), F32),
        grid=(r // tm,),
        in_specs=[rows(oa.shape[1]), rows(ob.shape[1]), rows(oc.shape[1])] + gates
        + [full(wa), full(wb), full(wc), full(wo), rows(d), pl.BlockSpec((1, d), lambda i: (0, 0))],
        out_specs=rows(d),
        compiler_params=_cparams(("parallel",)),
        name="merge_branches",
    )(oa, ob, oc, proj, proj, proj, wa, wb, wc, wo, x, gate)


def _topk_ranks(s):
    n_rows = s.shape[0]
    prio = (n_rows - lax.broadcasted_iota(jnp.int32, s.shape, 0)).astype(F32)
    rank = jnp.full(s.shape, float(PEER_TOPK), F32)
    vals = []
    for r in range(PEER_TOPK):
        m = jnp.max(s, axis=0, keepdims=True)
        tied = jnp.where(s == m, prio, 0.0)
        sel = tied == jnp.max(tied, axis=0, keepdims=True)
        rank = jnp.where(sel, float(r), rank)
        s = jnp.where(sel, -jnp.inf, s)
        vals.append(m)
    return rank, vals


def _peer_select_kernel(x_ref, g_ref, sh_ref, sc_ref, wq_ref, keys_ref,
                        ht_ref, n1_ref, a_ref, rk2_ref, b_ref):
    h2 = _norm_mod(x_ref[...], g_ref[...], sh_ref[...], sc_ref[...])
    h2t = h2.T.astype(BF16)
    ht_ref[...] = h2t
    qt = _dot(wq_ref[...], h2t)
    k = PEER_TOPK
    for h in range(PEER_HEADS):
        sv = []
        for p in range(2):
            hp = 2 * h + p
            sv.append(jnp.dot(keys_ref[hp], qt[hp * PEER_NKEYS:(hp + 1) * PEER_NKEYS, :],
                              preferred_element_type=F32, precision=lax.Precision.HIGHEST))
        rank1, d1 = _topk_ranks(sv[0])
        rank2, d2 = _topk_ranks(sv[1])
        hk = k // 2
        d2_all = jnp.concatenate(d2, axis=0)
        d2_low = d2_all[:hk]
        cand = jnp.concatenate([d1[0] + d2_all] + [d1[a] + d2_low for a in range(1, hk)]
                               + [jnp.concatenate(d1[hk:], axis=0) + d2[0]], axis=0)
        rank_c, _ = _topk_ranks(cand)
        chosen = rank_c < float(k)
        c_max = d1[0] + d2[0]
        z = jnp.sum(jnp.where(chosen, jnp.exp(cand - c_max), 0.0), axis=0, keepdims=True)
        picked = jnp.where(chosen, 1.0, 0.0)
        counts = [jnp.sum(picked[0:k], axis=0, keepdims=True)]
        counts += [jnp.sum(picked[k + hk * (a - 1):k + hk * a], axis=0, keepdims=True)
                   for a in range(1, hk)]
        counts += [picked[k + hk * (hk - 1) + a:k + hk * (hk - 1) + a + 1] for a in range(hk)]
        n1 = jnp.zeros(rank1.shape, F32)
        for a in range(k):
            n1 = n1 + jnp.where(rank1 == float(a), counts[a], 0.0)
        n1_ref[h] = n1
        a_ref[h] = jnp.exp(sv[0] - d1[0]) / z
        rk2_ref[h] = rank2.astype(BF16)
        b_ref[h] = jnp.exp(sv[1] - d2[0]).astype(BF16)


def _peer_select(x, g, sh, sc, wq_t, keys, tm):
    r, d = x.shape
    vec = pl.BlockSpec((1, d), lambda i: (0, 0))
    tab = jax.ShapeDtypeStruct((PEER_HEADS, PEER_NKEYS, r), F32)
    tab16 = jax.ShapeDtypeStruct((PEER_HEADS, PEER_NKEYS, r), BF16)
    tab_spec = pl.BlockSpec((PEER_HEADS, PEER_NKEYS, tm), lambda i: (0, 0, i))
    return pl.pallas_call(
        _peer_select_kernel,
        out_shape=(jax.ShapeDtypeStruct((d, r), BF16), tab, tab, tab16, tab16),
        grid=(r // tm,),
        in_specs=[pl.BlockSpec((tm, d), lambda i: (i, 0)), vec, vec, vec,
                  pl.BlockSpec(wq_t.shape, lambda i: (0, 0)),
                  pl.BlockSpec(keys.shape, lambda i: (0, 0, 0))],
        out_specs=(pl.BlockSpec((d, tm), lambda i: (0, i)), tab_spec, tab_spec, tab_spec, tab_spec),
        compiler_params=_cparams(("parallel",)),
        name="peer_select",
    )(x, g, sh, sc, wq_t, keys)


def _peer_expert_kernel(ht_ref, u_ref, vt_ref, n1_ref, a_ref, rk2_ref, b_ref, x_ref, gate_ref,
                        o_ref, acc_scr, c_scr, *, n_chunk):
    kk = pl.program_id(1)

    @pl.when(kk == 0)
    def _():
        acc_scr[...] = jnp.zeros(acc_scr.shape, F32)

    tm = ht_ref.shape[1]
    nk = PEER_NKEYS
    per_stage = PEER_STAGE // nk
    n_stage = n_chunk // per_stage

    def act_dot(s):
        return _dot(u_ref[s * PEER_STAGE:(s + 1) * PEER_STAGE, :], ht_ref[...])

    act_next = act_dot(0)
    for s in range(n_stage):
        act_t = act_next
        if s + 1 < n_stage:
            act_next = act_dot(s + 1)
        for cl in range(per_stage):
            c = s * per_stage + cl
            for l0 in range(0, tm, LANE):
                lanes = slice(l0, l0 + LANE)
                coef = None
                for h in range(PEER_HEADS):
                    n = jnp.broadcast_to(n1_ref[h, c:c + 1, lanes], (nk, LANE)).astype(BF16)
                    a = jnp.broadcast_to(a_ref[h, c:c + 1, lanes], (nk, LANE)).astype(BF16)
                    gate = jnp.minimum(jnp.maximum(n - rk2_ref[h, :, lanes], 0.0), a)
                    term = gate * b_ref[h, :, lanes]
                    coef = term if coef is None else coef + term
                act = act_t[cl * nk:(cl + 1) * nk, lanes]
                c_scr[c * nk:(c + 1) * nk, lanes] = jax.nn.gelu(act).astype(BF16) * coef
        rows = slice(s * PEER_STAGE, (s + 1) * PEER_STAGE)
        acc_scr[...] += _dot(vt_ref[:, rows], c_scr[rows, :])

    @pl.when(kk == pl.num_programs(1) - 1)
    def _():
        o_ref[...] = x_ref[...] + gate_ref[...] * acc_scr[...].T


def _peer_experts(h_t, n1, a, rk2, b, u, v_t, x, gate, tm, ec):
    d, r = h_t.shape
    e = u.shape[0]
    n_chunk = ec // PEER_NKEYS
    assert n_chunk % 8 == 0 and ec % PEER_STAGE == 0
    tab_spec = pl.BlockSpec((PEER_HEADS, PEER_NKEYS, tm), lambda i, k: (0, 0, i))
    row_spec = pl.BlockSpec((PEER_HEADS, n_chunk, tm), lambda i, k: (0, k, i))
    return pl.pallas_call(
        functools.partial(_peer_expert_kernel, n_chunk=n_chunk),
        out_shape=jax.ShapeDtypeStruct((r, d), F32),
        grid=(r // tm, e // ec),
        in_specs=[pl.BlockSpec((d, tm), lambda i, k: (0, i)),
                  pl.BlockSpec((ec, d), lambda i, k: (k, 0)),
                  pl.BlockSpec((d, ec), lambda i, k: (0, k)),
                  row_spec, row_spec, tab_spec, tab_spec,
                  pl.BlockSpec((tm, d), lambda i, k: (i, 0)),
                  pl.BlockSpec((1, d), lambda i, k: (0, 0))],
        out_specs=pl.BlockSpec((tm, d), lambda i, k: (i, 0)),
        scratch_shapes=[pltpu.VMEM((d, tm), F32), pltpu.VMEM((ec, tm), BF16)],
        compiler_params=_cparams(("parallel", "arbitrary")),
        name="peer_experts",
    )(h_t, u, v_t, n1, a, rk2, b, x, gate)


def _final_kernel(x_ref, g_ref, o_ref):
    x = x_ref[...]
    ms = jnp.mean(x * x, axis=-1, keepdims=True)
    o_ref[...] = x * lax.rsqrt(ms + NORM_EPS) * g_ref[...]


def _final_norm(x, g, tm):
    r, d = x.shape
    return pl.pallas_call(
        _final_kernel,
        out_shape=jax.ShapeDtypeStruct((r, d), F32),
        grid=(r // tm,),
        in_specs=[pl.BlockSpec((tm, d), lambda i: (i, 0)), pl.BlockSpec((1, d), lambda i: (0, 0))],
        out_specs=pl.BlockSpec((tm, d), lambda i: (i, 0)),
        compiler_params=_cparams(("parallel",)),
        name="final_norm",
    )(x, g)


def _rope_tables(s_len):
    quarter = HEAD_DIM // 4
    t = jnp.arange(s_len)
    pos_row = (t // GRID_W).astype(F32)
    pos_col = (t % GRID_W).astype(F32)
    inv_freq = ROPE_THETA ** (-jnp.arange(quarter, dtype=F32) / quarter)
    ar = pos_row[:, None] * inv_freq[None, :]
    ac = pos_col[:, None] * inv_freq[None, :]
    cos = jnp.concatenate([jnp.cos(ar), jnp.cos(ar), jnp.cos(ac), jnp.cos(ac)], axis=1)
    sin = jnp.concatenate([-jnp.sin(ar), jnp.sin(ar), -jnp.sin(ac), jnp.sin(ac)], axis=1)
    return cos, sin


def _local_geometry(rows):
    b = LOCAL_BLOCK
    rows_per_block = b // GRID_W
    kh = min(NA_KH, rows)
    q = np.arange(b)
    k = np.arange(3 * b)
    q_row_l, q_col = q // GRID_W, q % GRID_W
    k_row_l, k_col = k // GRID_W - rows_per_block, k % GRID_W
    nb = rows // rows_per_block
    geo = []
    for block in (0, 1, nb - 1):
        q_row = block * rows_per_block + q_row_l
        k_row = block * rows_per_block + k_row_l
        geo.append((q_row, q_col, k_row, k_col))
    return kh, geo


def _na_bias(rpb, rows):
    kh, geo = _local_geometry(rows)
    rpb_rows = LOCAL_BLOCK // GRID_W
    qr, kr = np.arange(rpb_rows), np.arange(3 * rpb_rows) - rpb_rows
    row_sel = (kr[None, :, None] - qr[:, None, None] + (NA_KH - 1)
               == np.arange(2 * NA_KH - 1)[None, None, :]).astype(np.float32)
    col = np.arange(GRID_W)
    col_sel = (col[None, :, None] - col[:, None, None] + (NA_KW - 1)
               == np.arange(2 * NA_KW - 1)[None, None, :]).astype(np.float32)
    hi = lax.Precision.HIGHEST
    t1 = jnp.einsum('hab,cdb->hacd', rpb.astype(F32), col_sel, precision=hi)
    bias = jnp.einsum('hacd,qka->hqckd', t1, row_sel, precision=hi)
    bias = bias.reshape(rpb.shape[0], LOCAL_BLOCK, 3 * LOCAL_BLOCK)
    out = []
    for q_row, q_col, k_row, k_col in geo:
        rs = np.clip(q_row - kh // 2, 0, rows - kh)
        cs = np.clip(q_col - NA_KW // 2, 0, GRID_W - NA_KW)
        ok = ((k_row[None, :] >= rs[:, None]) & (k_row[None, :] < rs[:, None] + kh)
              & (k_col[None, :] >= cs[:, None]) & (k_col[None, :] < cs[:, None] + NA_KW)
              & (k_row[None, :] >= 0) & (k_row[None, :] < rows))
        out.append(jnp.where(jnp.asarray(ok)[None], bias, NEG))
    return jnp.stack(out, axis=0)


def _window_mask(rows):
    _, geo = _local_geometry(rows)
    out = []
    for q_row, q_col, k_row, k_col in geo:
        qp = q_row * GRID_W + q_col
        kp = k_row * GRID_W + k_col
        ok = (np.abs(kp[None, :] - qp[:, None]) <= WINDOW) & (kp[None, :] >= 0) \
            & (kp[None, :] < rows * GRID_W)
        out.append(np.where(ok, 0.0, NEG)[None])
    return jnp.asarray(np.stack(out, axis=0), F32)


def _row_tile(r, pref):
    return pref if r % pref == 0 else LOCAL_BLOCK


def kernel(x, c, ctx, c_ctx, w_mod, b_mod, norm1_g, norm2_g, w_in, rpb_a, qnorm_b, knorm_b, sink_c,
           w_br_a, w_br_b, w_br_c, w_out, peer_wq, peer_keys, peer_u, peer_v, final_g):
    _, s_len, d = x.shape
    c_len = ctx.shape[1]
    depth = w_mod.shape[0]
    rows = s_len // GRID_W
    xs = x[0]
    xc = ctx[0]

    cos, sin = _rope_tables(s_len)
    cos_c = jnp.ones((c_len, HEAD_DIM), F32)
    sin_c = jnp.zeros((c_len, HEAD_DIM), F32)
    win_mask = _window_mask(rows)
    no_sink_a = jnp.full((NA_HEADS,), NEG, F32)
    no_sink_b = jnp.full((GA_HEADS,), NEG, F32)
    c8 = jnp.zeros((8, d), F32).at[0].set(c_ctx).at[1].set(c[0])

    g_blk = 3 * d // HEAD_DIM
    qa0, ka0, va0 = g_blk, g_blk + NA_HEADS, g_blk + 2 * NA_HEADS
    bc0 = g_blk + 3 * NA_HEADS
    vb0 = bc0 + GA_HEADS + GA_KV_HEADS
    vc0 = vb0 + GA_KV_HEADS + WA_HEADS + WA_KV_HEADS
    n_qkv = 3 * A_W + B_QW + 2 * B_KW + C_QW + 2 * C_KW

    tm_s = _row_tile(s_len, 512)
    gb = GA_HEADS // GA_KV_HEADS
    gc = WA_HEADS // WA_KV_HEADS
    for l in range(depth):
        ctx_out = l < depth - 1
        w_in_l = jnp.concatenate([w_in[l][:, n_qkv:], w_in[l][:, :n_qkv]], axis=1).astype(BF16)
        mod = _mod_vectors(c8, w_mod[l], b_mod[l])
        mc = [mod[0:1, t * d:(t + 1) * d] for t in range(N_MOD)]
        ms = [mod[1:2, t * d:(t + 1) * d] for t in range(N_MOD)]
        g1 = norm1_g[l].reshape(1, d)
        g2 = norm2_g[l].reshape(1, d)

        proj = _norm_proj(xs, g1, ms[0], ms[1], w_in_l, tm_s, 3584)
        proj_c = _norm_proj(xc, g1, mc[0], mc[1], w_in_l, c_len, 1536)
        qn = qnorm_b[l].reshape(1, HEAD_DIM)
        kn = knorm_b[l].reshape(1, HEAD_DIM)
        bc_block0 = bc0 // 6
        rope = _rope_heads(proj, cos, sin, qn, kn, tm_s, bc_block0)
        rope_c = _rope_heads(proj_c, cos_c, sin_c, qn, kn, c_len, bc_block0)

        na_scale = HEAD_DIM ** -0.5
        o_a = _local_attention(proj, qa0, proj, ka0, proj, va0, proj_c, ka0, proj_c, va0,
                               _na_bias(rpb_a[l], rows), no_sink_a, NA_HEADS, 1, na_scale)
        o_b = _flash_attention(rope, RO_QB, rope, RO_KB, proj, vb0, rope_c, RO_KB, proj_c, vb0,
                               no_sink_b, GA_HEADS, gb, 1.0, tm_s, _row_tile(s_len, 2048), True)
        o_c = _local_attention(rope, RO_QC, rope, RO_KC, proj, vc0, rope_c, RO_KC, proj_c, vc0,
                               win_mask, sink_c[l], WA_HEADS, gc, 1.0)
        wa = w_br_a[l].astype(BF16)
        wb = w_br_b[l].astype(BF16)
        wc = w_br_c[l].astype(BF16)
        wo = w_out[l].astype(BF16)
        x_mid = _merge(o_a, o_b, o_c, proj, wa, wb, wc, wo, xs, ms[2], LOCAL_BLOCK)

        wq_t = peer_wq[l].T.astype(BF16)
        keys = peer_keys[l].reshape(2 * PEER_HEADS, PEER_NKEYS, -1)
        u = peer_u[l].astype(BF16)
        v_t = peer_v[l].T.astype(BF16)
        sel = _peer_select(x_mid, g2, ms[3], ms[4], wq_t, keys, LOCAL_BLOCK)
        xs = _peer_experts(*sel, u=u, v_t=v_t, x=x_mid, gate=ms[5], tm=tm_s,
                           ec=PEER_EXPERT_CHUNK)

        if ctx_out:
            oc_a = _flash_attention(proj_c, qa0, proj_c, ka0, proj_c, va0, proj_c, ka0, proj_c,
                                    va0, no_sink_a, NA_HEADS, 1, na_scale * LOG2E, c_len, c_len,
                                    False)
            oc_b = _flash_attention(rope_c, RO_QB, rope_c, RO_KB, proj_c, vb0, rope_c, RO_KB,
                                    proj_c, vb0, no_sink_b, GA_HEADS, gb, 1.0, c_len, c_len, False)
            oc_c = _flash_attention(rope_c, RO_QC, rope_c, RO_KC, proj_c, vc0, rope_c, RO_KC,
                                    proj_c, vc0, sink_c[l] * LOG2E, WA_HEADS, gc, LOG2E, c_len,
                                    c_len, False)
            xc_mid = _merge(oc_a, oc_b, oc_c, proj_c, wa, wb, wc, wo, xc, mc[2], c_len)
            sel_c = _peer_select(xc_mid, g2, mc[3], mc[4], wq_t, keys, c_len)
            xc = _peer_experts(*sel_c, u=u, v_t=v_t, x=xc_mid, gate=mc[5], tm=c_len,
                               ec=PEER_EXPERT_CHUNK)

    return _final_norm(xs, final_g.reshape(1, d), tm_s).reshape(1, s_len, d)
```
